```python
import math
import jax, jax.numpy as jnp
from jax import lax
import numpy as np

D_MODEL = 1024
BATCH = 8
SEQ = 2048
DEPTH = 4
DEC_BATCH = 128
DEC_SEQ = 4
PAST_LEN = 2048
PAGE_SIZE = 128

N_REC_LAYERS = (DEPTH + 1) // 2
N_ATT_LAYERS = DEPTH // 2
D_FF = ((8 * D_MODEL // 3) + 127) // 128 * 128
W_LRU = D_MODEL // 2
LRU_HEADS = 8
LRU_HEAD_DIM = W_LRU // LRU_HEADS
CONV_W = 4
LRU_C = 8.0
W_S5 = D_MODEL // 2
S5_GROUP = 16
S5_GROUPS = W_S5 // S5_GROUP
S5_STATE = 64
N_HEADS = 16
HEAD_DIM = D_MODEL // N_HEADS
MOBA_BLOCK = 256
MOBA_TOPK = 3
MOBA_Q_BLOCK = 64
REL_BUCKETS = 32
REL_MAX_DIST = 128
EPS = 1e-6

kernel_name = 'hybrid_rglru_s5_moba_macaron_step'


def rms_norm(x, g):
    xf = x.astype(jnp.float32)
    y = xf * lax.rsqrt(jnp.mean(xf * xf, axis=-1, keepdims=True) + EPS)
    return (y * g.astype(jnp.float32)).astype(x.dtype)


def swiglu(h, wg, wu, wd):
    return (jax.nn.silu(h @ wg) * (h @ wu)) @ wd


def causal_conv(x, buf, w, b):
    t_len = x.shape[1]
    xx = jnp.concatenate([buf.astype(x.dtype), x], axis=1)
    y = b + sum(xx[:, j:j + t_len] * w[j] for j in range(CONV_W))
    return y, xx[:, -(CONV_W - 1):]


def rg_lru(xc, h0, w_r, b_r, w_i, b_i, lam):
    bsz, t_len, _ = xc.shape
    xh = xc.reshape(bsz, t_len, LRU_HEADS, LRU_HEAD_DIM)
    r = jax.nn.sigmoid(jnp.einsum('bthi,hij->bthj', xh, w_r).reshape(bsz, t_len, W_LRU) + b_r)
    i = jax.nn.sigmoid(jnp.einsum('bthi,hij->bthj', xh, w_i).reshape(bsz, t_len, W_LRU) + b_i)
    log_a = LRU_C * r.astype(jnp.float32) * jax.nn.log_sigmoid(lam.astype(jnp.float32))
    a = jnp.exp(log_a)
    u = jnp.sqrt(-jnp.expm1(2.0 * log_a)) * (i * xc).astype(jnp.float32)

    def step(h, inp):
        a_t, u_t = inp
        h = a_t * h + u_t
        return h, h

    h_last, hs = lax.scan(step, h0.astype(jnp.float32), (jnp.swapaxes(a, 0, 1), jnp.swapaxes(u, 0, 1)))
    return jnp.swapaxes(hs, 0, 1), h_last


def s5_ssm(u, h0_re, h0_im, lam_re, lam_im, log_step, b_re, b_im, c_re, c_im, d):
    f32 = jnp.float32
    bsz, t_len, _ = u.shape
    dt = jnp.exp(log_step.astype(f32))[:, None]
    lr, li = lam_re.astype(f32), lam_im.astype(f32)
    mag = jnp.exp(lr * dt)
    ar, ai = mag * jnp.cos(li * dt), mag * jnp.sin(li * dt)
    den = lr * lr + li * li
    qr = ((ar - 1.0) * lr + ai * li) / den
    qi = (ai * lr - (ar - 1.0) * li) / den
    br, bi = b_re.astype(f32), b_im.astype(f32)
    bbr = qr[..., None] * br - qi[..., None] * bi
    bbi = qr[..., None] * bi + qi[..., None] * br
    ug = u.reshape(bsz, t_len, S5_GROUPS, S5_GROUP).astype(f32)
    bu_r = jnp.einsum('btgi,gpi->btgp', ug, bbr)
    bu_i = jnp.einsum('btgi,gpi->btgp', ug, bbi)
    h0r, h0i = h0_re.astype(f32), h0_im.astype(f32)
    bu_r = bu_r.at[:, 0].add(ar * h0r - ai * h0i)
    bu_i = bu_i.at[:, 0].add(ar * h0i + ai * h0r)

    def combine(e1, e2):
        a1r, a1i, b1r, b1i = e1
        a2r, a2i, b2r, b2i = e2
        return (a2r * a1r - a2i * a1i, a2r * a1i + a2i * a1r,
                a2r * b1r - a2i * b1i + b2r, a2r * b1i + a2i * b1r + b2i)

    _, _, hr, hi = lax.associative_scan(
        combine, (jnp.broadcast_to(ar, bu_r.shape), jnp.broadcast_to(ai, bu_r.shape), bu_r, bu_i), axis=1)
    y = (jnp.einsum('btgp,gip->btgi', hr, c_re.astype(f32))
         - jnp.einsum('btgp,gip->btgi', hi, c_im.astype(f32))).reshape(bsz, t_len, W_S5)
    y = y + d.astype(f32) * u.astype(f32)
    return y, hr[:, -1], hi[:, -1]


def recurrent_mixer(h, conv_buf, lru_h0, s5_h0r, s5_h0i, w_in, conv_w, conv_b, w_r, b_r, w_i, b_i, lam,
                    lam_re, lam_im, log_step, b_re, b_im, c_re, c_im, d, glu_w, glu_b, w_out):
    z = h @ w_in
    xa, ga, ub = z[..., :W_LRU], z[..., W_LRU:2 * W_LRU], z[..., 2 * W_LRU:]
    xc, conv_new = causal_conv(xa, conv_buf, conv_w, conv_b)
    hs, lru_last = rg_lru(xc, lru_h0, w_r, b_r, w_i, b_i, lam)
    ya = hs.astype(h.dtype) * jax.nn.gelu(ga)
    ys5, s5r, s5i = s5_ssm(ub, s5_h0r, s5_h0i, lam_re, lam_im, log_step, b_re, b_im, c_re, c_im, d)
    ys5 = jax.nn.gelu(ys5).astype(h.dtype)
    yb = ys5 * jax.nn.sigmoid(ys5 @ glu_w + glu_b)
    out = jnp.concatenate([ya, yb], axis=-1) @ w_out
    return out, conv_new, lru_last, s5r, s5i


def t5_bucket(dist):
    max_exact = REL_BUCKETS // 2
    log_ratio = jnp.log(jnp.maximum(dist, 1).astype(jnp.float32) / max_exact) / math.log(REL_MAX_DIST / max_exact)
    large = jnp.minimum(max_exact + (log_ratio * (REL_BUCKETS - max_exact)).astype(jnp.int32), REL_BUCKETS - 1)
    return jnp.where(dist < max_exact, dist, large)


def to_blocks(k, n_blocks):
    k = jnp.pad(k, ((0, n_blocks * MOBA_BLOCK - k.shape[0]), (0, 0), (0, 0)))
    return jnp.transpose(k.reshape(n_blocks, MOBA_BLOCK, N_HEADS, HEAD_DIM), (2, 0, 1, 3))


def moba_attend(q, qpos, kbh, vbh, kmean, rel_bias):
    nq = q.shape[0]
    nbp = kbh.shape[1]
    cur = qpos // MOBA_BLOCK
    gate = jnp.einsum('qhd,hnd->qhn', q.astype(jnp.float32), kmean)
    gate = jnp.where(jnp.arange(nbp)[None, None, :] < cur[:, None, None], gate, -jnp.inf)
    _, top = lax.top_k(gate, MOBA_TOPK)
    cur_b = jnp.broadcast_to(cur[:, None, None], (nq, N_HEADS, 1))
    sel = jnp.concatenate([top, cur_b], axis=-1)
    slot_ok = jnp.concatenate([top < cur_b, jnp.ones(cur_b.shape, dtype=bool)], axis=-1)
    hidx = jnp.arange(N_HEADS)[None, :, None]
    kg = kbh[hidx, sel]
    vg = vbh[hidx, sel]
    kpos = sel[..., None] * MOBA_BLOCK + jnp.arange(MOBA_BLOCK)
    dist = qpos[:, None, None, None] - kpos
    valid = slot_ok[..., None] & (dist >= 0)
    bias = rel_bias.T.astype(jnp.float32)[hidx[..., None], t5_bucket(jnp.maximum(dist, 0))]
    s = jnp.einsum('qhd,qhsld->qhsl', q, kg, preferred_element_type=jnp.float32) * (HEAD_DIM ** -0.5) + bias
    s = jnp.where(valid, s, -jnp.inf).reshape(nq, N_HEADS, -1)
    p = jax.nn.softmax(s, axis=-1).reshape(kpos.shape)
    o = jnp.einsum('qhsl,qhsld->qhd', p.astype(vg.dtype), vg, preferred_element_type=jnp.float32)
    return o.astype(q.dtype)


def moba_prompt(q, k, v, rel_bias):
    bsz, s_len = q.shape[:2]
    nb = max(-(-s_len // MOBA_BLOCK), MOBA_TOPK)
    kbh = jax.vmap(lambda t: to_blocks(t, nb))(k)
    vbh = jax.vmap(lambda t: to_blocks(t, nb))(v)
    kmean = kbh.astype(jnp.float32).mean(axis=3)
    n_chunks = s_len // MOBA_Q_BLOCK
    qc = q.reshape(bsz * n_chunks, MOBA_Q_BLOCK, N_HEADS, HEAD_DIM)

    def one(args):
        qq, it = args
        b = it // n_chunks
        qpos = (it % n_chunks) * MOBA_Q_BLOCK + jnp.arange(MOBA_Q_BLOCK)
        return moba_attend(qq, qpos, kbh[b], vbh[b], kmean[b], rel_bias)

    return lax.map(one, (qc, jnp.arange(bsz * n_chunks))).reshape(q.shape)


def moba_sample(q, k_new, v_new, cache_k, cache_v, layer, page_table, rel_bias):
    t_new = q.shape[1]
    past = page_table.shape[1] * PAGE_SIZE
    nb = max(-(-(past + t_new) // MOBA_BLOCK), MOBA_TOPK)
    qpos = past + jnp.arange(t_new)

    def one(args):
        qq, kn, vn, pt = args
        kk = jnp.concatenate([cache_k[layer, pt].reshape(past, N_HEADS, HEAD_DIM), kn.astype(cache_k.dtype)], axis=0)
        vv = jnp.concatenate([cache_v[layer, pt].reshape(past, N_HEADS, HEAD_DIM), vn.astype(cache_v.dtype)], axis=0)
        kbh = to_blocks(kk, nb)
        vbh = to_blocks(vv, nb)
        return moba_attend(qq, qpos, kbh, vbh, kbh.astype(jnp.float32).mean(axis=2), rel_bias)

    return lax.map(one, (q, k_new, v_new, page_table))


def qkv_proj(h, w_qkv, q_g, k_g):
    bsz, t_len, _ = h.shape
    qkv = (h @ w_qkv).reshape(bsz, t_len, 3, N_HEADS, HEAD_DIM)
    return rms_norm(qkv[:, :, 0], q_g), rms_norm(qkv[:, :, 1], k_g), qkv[:, :, 2]


def setup_inputs(seed: int = 0) -> dict:
    key = jax.random.key(seed)
    keys = iter(jax.random.split(key, 64))
    f32 = jnp.float32

    def nrm(shape, scale):
        return jax.random.normal(next(keys), shape, f32) * scale

    n_pages = PAST_LEN // PAGE_SIZE
    n_used = DEC_BATCH * n_pages
    n_pool = n_used + n_used // 4
    perm = jax.random.permutation(next(keys), n_pool)
    page_table = perm[:n_used].reshape(DEC_BATCH, n_pages).astype(jnp.int32)
    a8 = jax.random.uniform(next(keys), (N_REC_LAYERS, W_LRU), f32, 0.9, 0.999)
    sig = a8 ** (1.0 / LRU_C)
    return {
        'x_prompt': nrm((BATCH, SEQ, D_MODEL), 1.0),
        'x_sample': nrm((DEC_BATCH, DEC_SEQ, D_MODEL), 1.0),
        'cache_k': nrm((N_ATT_LAYERS, n_pool, PAGE_SIZE, N_HEADS, HEAD_DIM), 1.0),
        'cache_v': nrm((N_ATT_LAYERS, n_pool, PAGE_SIZE, N_HEADS, HEAD_DIM), 1.0),
        'state_lru_h': nrm((N_REC_LAYERS, DEC_BATCH, W_LRU), 0.5),
        'state_conv': nrm((N_REC_LAYERS, DEC_BATCH, CONV_W - 1, W_LRU), 1.0),
        'state_s5_re': nrm((N_REC_LAYERS, DEC_BATCH, S5_GROUPS, S5_STATE), 1.0),
        'state_s5_im': nrm((N_REC_LAYERS, DEC_BATCH, S5_GROUPS, S5_STATE), 1.0),
        'page_table': page_table,
        'norm_g': 1.0 + nrm((DEPTH, 3, D_MODEL), 0.01),
        'ffn_w_gate': nrm((DEPTH, 2, D_MODEL, D_FF), D_MODEL ** -0.5),
        'ffn_w_up': nrm((DEPTH, 2, D_MODEL, D_FF), D_MODEL ** -0.5),
        'ffn_w_down': nrm((DEPTH, 2, D_FF, D_MODEL), D_FF ** -0.5),
        'rec_w_in': nrm((N_REC_LAYERS, D_MODEL, 2 * W_LRU + W_S5), D_MODEL ** -0.5),
        'rec_conv_w': nrm((N_REC_LAYERS, CONV_W, W_LRU), CONV_W ** -0.5),
        'rec_conv_b': nrm((N_REC_LAYERS, W_LRU), 0.01),
        'lru_w_r': nrm((N_REC_LAYERS, LRU_HEADS, LRU_HEAD_DIM, LRU_HEAD_DIM), LRU_HEAD_DIM ** -0.5),
        'lru_b_r': nrm((N_REC_LAYERS, W_LRU), 0.01),
        'lru_w_i': nrm((N_REC_LAYERS, LRU_HEADS, LRU_HEAD_DIM, LRU_HEAD_DIM), LRU_HEAD_DIM ** -0.5),
        'lru_b_i': nrm((N_REC_LAYERS, W_LRU), 0.01),
        'lru_lambda': jnp.log(sig) - jnp.log1p(-sig),
        's5_lambda_re': -0.5 + nrm((N_REC_LAYERS, S5_GROUPS, S5_STATE), 0.01),
        's5_lambda_im': math.pi * jnp.arange(S5_STATE, dtype=f32) + nrm((N_REC_LAYERS, S5_GROUPS, S5_STATE), 0.01),
        's5_log_step': jax.random.uniform(next(keys), (N_REC_LAYERS, S5_GROUPS), f32, math.log(0.001), math.log(0.1)),
        's5_b_re': nrm((N_REC_LAYERS, S5_GROUPS, S5_STATE, S5_GROUP), (2 * S5_GROUP) ** -0.5),
        's5_b_im': nrm((N_REC_LAYERS, S5_GROUPS, S5_STATE, S5_GROUP), (2 * S5_GROUP) ** -0.5),
        's5_c_re': nrm((N_REC_LAYERS, S5_GROUPS, S5_GROUP, S5_STATE), (2 * S5_STATE) ** -0.5),
        's5_c_im': nrm((N_REC_LAYERS, S5_GROUPS, S5_GROUP, S5_STATE), (2 * S5_STATE) ** -0.5),
        's5_d': nrm((N_REC_LAYERS, W_S5), 1.0),
        's5_glu_w': nrm((N_REC_LAYERS, W_S5, W_S5), W_S5 ** -0.5),
        's5_glu_b': nrm((N_REC_LAYERS, W_S5), 0.01),
        'rec_w_out': nrm((N_REC_LAYERS, W_LRU + W_S5, D_MODEL), (W_LRU + W_S5) ** -0.5),
        'att_w_qkv': nrm((N_ATT_LAYERS, D_MODEL, 3 * N_HEADS * HEAD_DIM), D_MODEL ** -0.5),
        'att_q_norm': 1.0 + nrm((N_ATT_LAYERS, HEAD_DIM), 0.01),
        'att_k_norm': 1.0 + nrm((N_ATT_LAYERS, HEAD_DIM), 0.01),
        'att_w_o': nrm((N_ATT_LAYERS, N_HEADS * HEAD_DIM, D_MODEL), (N_HEADS * HEAD_DIM) ** -0.5),
        'rel_bias': nrm((REL_BUCKETS, N_HEADS), 0.5),
    }


def reference(x_prompt, x_sample, cache_k, cache_v, state_lru_h, state_conv, state_s5_re, state_s5_im, page_table,
              norm_g, ffn_w_gate, ffn_w_up, ffn_w_down, rec_w_in, rec_conv_w, rec_conv_b, lru_w_r, lru_b_r, lru_w_i,
              lru_b_i, lru_lambda, s5_lambda_re, s5_lambda_im, s5_log_step, s5_b_re, s5_b_im, s5_c_re, s5_c_im, s5_d,
              s5_glu_w, s5_glu_b, rec_w_out, att_w_qkv, att_q_norm, att_k_norm, att_w_o, rel_bias):
    xp, xs = x_prompt, x_sample
    bsz = xp.shape[0]
    conv0 = jnp.zeros((bsz, CONV_W - 1, W_LRU), xp.dtype)
    lru0 = jnp.zeros((bsz, W_LRU), jnp.float32)
    s50 = jnp.zeros((bsz, S5_GROUPS, S5_STATE), jnp.float32)
    k_p, v_p, k_s, v_s = [], [], [], []
    lru_p, lru_s, conv_p, conv_s, s5r_p, s5r_s, s5i_p, s5i_s = [], [], [], [], [], [], [], []
    for l in range(DEPTH):
        xp = xp + 0.5 * swiglu(rms_norm(xp, norm_g[l, 0]), ffn_w_gate[l, 0], ffn_w_up[l, 0], ffn_w_down[l, 0])
        xs = xs + 0.5 * swiglu(rms_norm(xs, norm_g[l, 0]), ffn_w_gate[l, 0], ffn_w_up[l, 0], ffn_w_down[l, 0])
        hp = rms_norm(xp, norm_g[l, 1])
        hs = rms_norm(xs, norm_g[l, 1])
        if l % 2 == 0:
            r = l // 2
            prm = (rec_w_in[r], rec_conv_w[r], rec_conv_b[r], lru_w_r[r], lru_b_r[r], lru_w_i[r], lru_b_i[r],
                   lru_lambda[r], s5_lambda_re[r], s5_lambda_im[r], s5_log_step[r], s5_b_re[r], s5_b_im[r],
                   s5_c_re[r], s5_c_im[r], s5_d[r], s5_glu_w[r], s5_glu_b[r], rec_w_out[r])
            yp, cp, lp, srp, sip = recurrent_mixer(hp, conv0, lru0, s50, s50, *prm)
            ys, cs, ls, srs, sis = recurrent_mixer(hs, state_conv[r], state_lru_h[r], state_s5_re[r],
                                                   state_s5_im[r], *prm)
            conv_p.append(cp); conv_s.append(cs); lru_p.append(lp); lru_s.append(ls)
            s5r_p.append(srp); s5r_s.append(srs); s5i_p.append(sip); s5i_s.append(sis)
        else:
            a = l // 2
            qp, kp, vp = qkv_proj(hp, att_w_qkv[a], att_q_norm[a], att_k_norm[a])
            op = moba_prompt(qp, kp, vp, rel_bias)
            yp = op.reshape(hp.shape[0], hp.shape[1], N_HEADS * HEAD_DIM) @ att_w_o[a]
            qs, ks, vs = qkv_proj(hs, att_w_qkv[a], att_q_norm[a], att_k_norm[a])
            os_ = moba_sample(qs, ks, vs, cache_k, cache_v, a, page_table, rel_bias)
            ys = os_.reshape(hs.shape[0], hs.shape[1], N_HEADS * HEAD_DIM) @ att_w_o[a]
            k_p.append(kp); v_p.append(vp); k_s.append(ks); v_s.append(vs)
        xp = xp + yp
        xs = xs + ys
        xp = xp + 0.5 * swiglu(rms_norm(xp, norm_g[l, 2]), ffn_w_gate[l, 1], ffn_w_up[l, 1], ffn_w_down[l, 1])
        xs = xs + 0.5 * swiglu(rms_norm(xs, norm_g[l, 2]), ffn_w_gate[l, 1], ffn_w_up[l, 1], ffn_w_down[l, 1])
    return (xp, xs, jnp.stack(k_p), jnp.stack(v_p), jnp.stack(k_s), jnp.stack(v_s),
            jnp.stack(lru_p), jnp.stack(lru_s), jnp.stack(conv_p), jnp.stack(conv_s),
            jnp.stack(s5r_p), jnp.stack(s5r_s), jnp.stack(s5i_p), jnp.stack(s5i_s))
```

```python
import functools
import math

import numpy as np
import jax
import jax.numpy as jnp
from jax import lax
from jax.experimental import pallas as pl
from jax.experimental.pallas import tpu as pltpu

f32 = jnp.float32
bf16 = jnp.bfloat16

D_MODEL = 1024
D_FF = 2816
W_LRU = 512
W_S5 = 512
LRU_HEADS = 8
LRU_HEAD_DIM = 64
CONV_W = 4
LRU_C = 8.0
S5_GROUP = 16
S5_GROUPS = 32
S5_STATE = 64
S5_N = S5_GROUPS * S5_STATE
N_HEADS = 16
HEAD_DIM = 64
N_PAIRS = N_HEADS // 2
MOBA_BLOCK = 256
MOBA_TOPK = 3
REL_BUCKETS = 32
REL_MAX_DIST = 128
PAGE_SIZE = 128
EPS = 1e-6

TM = 512
FF_CHUNK = 1408
NEG = -1e30
VMEM_LIMIT = 56 * 1024 * 1024

_DN_T = (((1,), (1,)), ((), ()))


def _cparams(*sem):
    return pltpu.CompilerParams(dimension_semantics=sem, vmem_limit_bytes=VMEM_LIMIT)


def _rms(x, g):
    ms = jnp.mean(x * x, axis=-1, keepdims=True)
    return x * lax.rsqrt(ms + EPS) * g


def _ffn_body(x_ref, g_ref, wg_ref, wu_ref, wd_ref, o_ref, xn_ref):
    k = pl.program_id(1)

    @pl.when(k == 0)
    def _():
        xn_ref[...] = _rms(x_ref[...], g_ref[...]).astype(bf16)

    xn = xn_ref[...]
    gate = jnp.dot(xn, wg_ref[...], preferred_element_type=f32)
    up = jnp.dot(xn, wu_ref[...], preferred_element_type=f32)
    h = (gate * jax.nn.sigmoid(gate) * up).astype(bf16)
    d = 0.5 * jnp.dot(h, wd_ref[...], preferred_element_type=f32)

    @pl.when(k == 0)
    def _():
        o_ref[...] = x_ref[...] + d

    @pl.when(k != 0)
    def _():
        o_ref[...] += d


def _ffn(x, g, wg, wu, wd):
    n = x.shape[0]
    return pl.pallas_call(
        _ffn_body,
        grid=(n // TM, D_FF // FF_CHUNK),
        in_specs=[
            pl.BlockSpec((TM, D_MODEL), lambda i, k: (i, 0)),
            pl.BlockSpec((1, D_MODEL), lambda i, k: (0, 0)),
            pl.BlockSpec((D_MODEL, FF_CHUNK), lambda i, k: (0, k)),
            pl.BlockSpec((D_MODEL, FF_CHUNK), lambda i, k: (0, k)),
            pl.BlockSpec((FF_CHUNK, D_MODEL), lambda i, k: (k, 0)),
        ],
        out_specs=pl.BlockSpec((TM, D_MODEL), lambda i, k: (i, 0)),
        out_shape=jax.ShapeDtypeStruct((n, D_MODEL), f32),
        scratch_shapes=[pltpu.VMEM((TM, D_MODEL), bf16)],
        compiler_params=_cparams("parallel", "arbitrary"),
        name="ffn",
    )(x, g, wg, wu, wd)


def _norm_proj_body(x_ref, g_ref, w_ref, o_ref):
    xn = _rms(x_ref[...], g_ref[...]).astype(bf16)
    o_ref[...] = jnp.dot(xn, w_ref[...], preferred_element_type=f32)


def _norm_proj(x, g, w):
    n, dout = x.shape[0], w.shape[1]
    return pl.pallas_call(
        _norm_proj_body,
        grid=(n // TM,),
        in_specs=[
            pl.BlockSpec((TM, D_MODEL), lambda i: (i, 0)),
            pl.BlockSpec((1, D_MODEL), lambda i: (0, 0)),
            pl.BlockSpec((D_MODEL, dout), lambda i: (0, 0)),
        ],
        out_specs=pl.BlockSpec((TM, dout), lambda i: (i, 0)),
        out_shape=jax.ShapeDtypeStruct((n, dout), f32),
        compiler_params=_cparams("parallel"),
        name="norm_proj",
    )(x, g, w)


def _att_out_body(x_ref, y_ref, w_ref, o_ref):
    o_ref[...] = x_ref[...] + jnp.dot(y_ref[...].astype(bf16), w_ref[...], preferred_element_type=f32)


def _att_out(x, y, w):
    n = x.shape[0]
    return pl.pallas_call(
        _att_out_body,
        grid=(n // TM,),
        in_specs=[
            pl.BlockSpec((TM, D_MODEL), lambda i: (i, 0)),
            pl.BlockSpec((TM, D_MODEL), lambda i: (i, 0)),
            pl.BlockSpec((D_MODEL, D_MODEL), lambda i: (0, 0)),
        ],
        out_specs=pl.BlockSpec((TM, D_MODEL), lambda i: (i, 0)),
        out_shape=jax.ShapeDtypeStruct((n, D_MODEL), f32),
        compiler_params=_cparams("parallel"),
        name="att_out",
    )(x, y, w)


def _rec_out_body(x_ref, y_ref, gw_ref, gb_ref, w_ref, o_ref):
    y = y_ref[...]
    ya = y[:, :W_LRU].astype(bf16)
    g5 = y[:, W_LRU:]
    g5b = g5.astype(bf16)
    glu = jax.nn.sigmoid(jnp.dot(g5b, gw_ref[...], preferred_element_type=f32) + gb_ref[...])
    yb = (g5 * glu).astype(bf16)
    yy = jnp.concatenate([ya, yb], axis=-1)
    o_ref[...] = x_ref[...] + jnp.dot(yy, w_ref[...], preferred_element_type=f32)


def _rec_out(x, y, glu_w, glu_b, w_out):
    n = x.shape[0]
    return pl.pallas_call(
        _rec_out_body,
        grid=(n // TM,),
        in_specs=[
            pl.BlockSpec((TM, D_MODEL), lambda i: (i, 0)),
            pl.BlockSpec((TM, D_MODEL), lambda i: (i, 0)),
            pl.BlockSpec((W_S5, W_S5), lambda i: (0, 0)),
            pl.BlockSpec((1, W_S5), lambda i: (0, 0)),
            pl.BlockSpec((D_MODEL, D_MODEL), lambda i: (0, 0)),
        ],
        out_specs=pl.BlockSpec((TM, D_MODEL), lambda i: (i, 0)),
        out_shape=jax.ShapeDtypeStruct((n, D_MODEL), f32),
        compiler_params=_cparams("parallel"),
        name="rec_out",
    )(x, y, glu_w, glu_b, w_out)


def _rec_scan_body(bsz, tc, z_ref, conv0_ref, lru0_ref, s5r0_ref, s5i0_ref, cw_ref, cb_ref, wr_ref, br_ref,
                   wi_ref, bi_ref, lam_ref, ar_ref, ai_ref, bbr_ref, bbi_ref, ccr_ref, cci_ref, d_ref,
                   y_ref, conv_out, lru_out, s5r_out, s5i_out,
                   xbuf, a_scr, u_scr, hr_scr, hi_scr, lru_c, s5r_c, s5i_c):
    rows = bsz * tc
    step = pl.program_id(0)
    last = pl.num_programs(0) - 1

    @pl.when(step == 0)
    def _():
        xbuf[0:3 * bsz, :] = conv0_ref[...]
        lru_c[...] = lru0_ref[...]
        s5r_c[...] = s5r0_ref[...]
        s5i_c[...] = s5i0_ref[...]

    xbuf[3 * bsz:, :] = z_ref[:, 0:W_LRU]
    xc = cb_ref[...] + cw_ref[0:1, :] * xbuf[0:rows, :]
    for j in range(1, CONV_W):
        xc = xc + cw_ref[j:j + 1, :] * xbuf[j * bsz:j * bsz + rows, :]
    tail = xbuf[rows:rows + 3 * bsz, :]

    xcb = xc.astype(bf16)
    r = jax.nn.sigmoid(jnp.dot(xcb, wr_ref[...], preferred_element_type=f32) + br_ref[...])
    ig = jax.nn.sigmoid(jnp.dot(xcb, wi_ref[...], preferred_element_type=f32) + bi_ref[...])
    lam = lam_ref[...]
    log_sig = jnp.minimum(lam, 0.0) - jnp.log1p(jnp.exp(-jnp.abs(lam)))
    log_a = (LRU_C * r) * log_sig
    a = jnp.exp(log_a)
    mult = jnp.sqrt(-jnp.tanh(log_a) * (a * a + 1.0))
    a_scr[...] = a
    u_scr[...] = mult * (ig * xc)

    def lru_step(t, h):
        rs = pl.ds(pl.multiple_of(t * bsz, 8), bsz)
        h = a_scr[rs, :] * h + u_scr[rs, :]
        u_scr[rs, :] = h
        return h

    h_last = lax.fori_loop(0, tc, lru_step, lru_c[...])
    lru_c[...] = h_last
    y_ref[:, 0:W_LRU] = u_scr[...] * jax.nn.gelu(z_ref[:, W_LRU:2 * W_LRU])

    half_in, half_st = W_S5 // 2, S5_N // 2
    for k in range(2):
        ubk = z_ref[:, 2 * W_LRU + k * half_in:2 * W_LRU + (k + 1) * half_in].astype(bf16)
        hr_scr[:, k * half_st:(k + 1) * half_st] = jnp.dot(ubk, bbr_ref[k], preferred_element_type=f32)
        hi_scr[:, k * half_st:(k + 1) * half_st] = jnp.dot(ubk, bbi_ref[k], preferred_element_type=f32)

    lane_chunk = 512
    for c in range(S5_N // lane_chunk):
        cs = slice(c * lane_chunk, (c + 1) * lane_chunk)
        ar = jnp.broadcast_to(ar_ref[:, cs], (bsz, lane_chunk))
        ai = jnp.broadcast_to(ai_ref[:, cs], (bsz, lane_chunk))

        def s5_step(t, carry, cs=cs, ar=ar, ai=ai):
            hr, hi = carry
            rs = pl.ds(pl.multiple_of(t * bsz, 8), bsz)
            nr = ar * hr - ai * hi + hr_scr[rs, cs]
            ni = ar * hi + ai * hr + hi_scr[rs, cs]
            hr_scr[rs, cs] = nr
            hi_scr[rs, cs] = ni
            return nr, ni

        hr, hi = lax.fori_loop(0, tc, s5_step, (s5r_c[:, cs], s5i_c[:, cs]))
        s5r_c[:, cs] = hr
        s5i_c[:, cs] = hi

    for k in range(2):
        hrk = hr_scr[:, k * half_st:(k + 1) * half_st].astype(bf16)
        hik = hi_scr[:, k * half_st:(k + 1) * half_st].astype(bf16)
        ys = (jnp.dot(hrk, ccr_ref[k], preferred_element_type=f32)
              - jnp.dot(hik, cci_ref[k], preferred_element_type=f32))
        ub = z_ref[:, 2 * W_LRU + k * half_in:2 * W_LRU + (k + 1) * half_in]
        ys = ys + d_ref[:, k * half_in:(k + 1) * half_in] * ub
        y_ref[:, W_LRU + k * half_in:W_LRU + (k + 1) * half_in] = jax.nn.gelu(ys)

    xbuf[0:3 * bsz, :] = tail

    @pl.when(step == last)
    def _():
        conv_out[...] = tail
        lru_out[...] = lru_c[...]
        s5r_out[...] = s5r_c[...]
        s5i_out[...] = s5i_c[...]


def _rec_scan(z, bsz, tc, conv0, lru0, s5r0, s5i0, prm):
    n = z.shape[0]
    rows = bsz * tc
    full = lambda shape: pl.BlockSpec(shape, lambda i: (0,) * len(shape))
    in_specs = [
        pl.BlockSpec((rows, 3 * W_LRU), lambda i: (i, 0)),
        full((3 * bsz, W_LRU)), full((bsz, W_LRU)), full((bsz, S5_N)), full((bsz, S5_N)),
        full((CONV_W, W_LRU)), full((1, W_LRU)),
        full((W_LRU, W_LRU)), full((1, W_LRU)), full((W_LRU, W_LRU)), full((1, W_LRU)), full((1, W_LRU)),
        full((1, S5_N)), full((1, S5_N)),
        full((2, W_S5 // 2, S5_N // 2)), full((2, W_S5 // 2, S5_N // 2)),
        full((2, S5_N // 2, W_S5 // 2)), full((2, S5_N // 2, W_S5 // 2)),
        full((1, W_S5)),
    ]
    out_shape = (
        jax.ShapeDtypeStruct((n, D_MODEL), f32),
        jax.ShapeDtypeStruct((3 * bsz, W_LRU), f32),
        jax.ShapeDtypeStruct((bsz, W_LRU), f32),
        jax.ShapeDtypeStruct((bsz, S5_N), f32),
        jax.ShapeDtypeStruct((bsz, S5_N), f32),
    )
    out_specs = (
        pl.BlockSpec((rows, D_MODEL), lambda i: (i, 0)),
        full((3 * bsz, W_LRU)), full((bsz, W_LRU)), full((bsz, S5_N)), full((bsz, S5_N)),
    )
    scratch = [
        pltpu.VMEM((rows + 3 * bsz, W_LRU), f32),
        pltpu.VMEM((rows, W_LRU), f32), pltpu.VMEM((rows, W_LRU), f32),
        pltpu.VMEM((rows, S5_N), f32), pltpu.VMEM((rows, S5_N), f32),
        pltpu.VMEM((bsz, W_LRU), f32), pltpu.VMEM((bsz, S5_N), f32), pltpu.VMEM((bsz, S5_N), f32),
    ]
    return pl.pallas_call(
        functools.partial(_rec_scan_body, bsz, tc),
        grid=(n // rows,),
        in_specs=in_specs,
        out_specs=out_specs,
        out_shape=out_shape,
        scratch_shapes=scratch,
        compiler_params=_cparams("arbitrary"),
        name="rec_scan",
    )(z, conv0, lru0, s5r0, s5i0, *prm)


def _head_norm_t(t, gain):
    n = t.shape[1]
    t3 = t.reshape(N_HEADS, HEAD_DIM, n)
    ms = jnp.mean(t3 * t3, axis=1, keepdims=True)
    return (t3 * lax.rsqrt(ms + EPS) * gain[None]).reshape(N_HEADS * HEAD_DIM, n)


def _qkv_body(x_ref, g_ref, wt_ref, qg_ref, kg_ref, qb_ref, qf_ref, kt_ref, vt_ref, km_ref):
    xn = _rms(x_ref[...], g_ref[...])
    xnt = xn.T.astype(bf16)
    qkvt = jnp.dot(wt_ref[...], xnt, preferred_element_type=f32)
    qt = _head_norm_t(qkvt[0:D_MODEL], qg_ref[...])
    kt = _head_norm_t(qkvt[D_MODEL:2 * D_MODEL], kg_ref[...])
    q = qt.T
    qf_ref[...] = q
    qb_ref[...] = (q * (HEAD_DIM ** -0.5)).astype(bf16)
    kt_ref[0] = kt
    vt_ref[0] = qkvt[2 * D_MODEL:]
    for n in range(TM // MOBA_BLOCK):
        km_ref[0, :, n:n + 1] = jnp.mean(kt[:, n * MOBA_BLOCK:(n + 1) * MOBA_BLOCK], axis=1, keepdims=True)


def _qkv(x, g, wt, qg, kg, nb):
    n = x.shape[0]
    tlen = n // nb
    tpb = tlen // TM
    return pl.pallas_call(
        _qkv_body,
        grid=(n // TM,),
        in_specs=[
            pl.BlockSpec((TM, D_MODEL), lambda i: (i, 0)),
            pl.BlockSpec((1, D_MODEL), lambda i: (0, 0)),
            pl.BlockSpec((3 * D_MODEL, D_MODEL), lambda i: (0, 0)),
            pl.BlockSpec((HEAD_DIM, TM), lambda i: (0, 0)),
            pl.BlockSpec((HEAD_DIM, TM), lambda i: (0, 0)),
        ],
        out_specs=(
            pl.BlockSpec((TM, D_MODEL), lambda i: (i, 0)),
            pl.BlockSpec((TM, D_MODEL), lambda i: (i, 0)),
            pl.BlockSpec((1, D_MODEL, TM), lambda i: (i // tpb, 0, i % tpb)),
            pl.BlockSpec((1, D_MODEL, TM), lambda i: (i // tpb, 0, i % tpb)),
            pl.BlockSpec((1, D_MODEL, TM // MOBA_BLOCK), lambda i: (i, 0, 0)),
        ),
        out_shape=(
            jax.ShapeDtypeStruct((n, D_MODEL), bf16),
            jax.ShapeDtypeStruct((n, D_MODEL), f32),
            jax.ShapeDtypeStruct((nb, D_MODEL, tlen), f32),
            jax.ShapeDtypeStruct((nb, D_MODEL, tlen), f32),
            jax.ShapeDtypeStruct((n // TM, D_MODEL, TM // MOBA_BLOCK), f32),
        ),
        compiler_params=_cparams("parallel"),
        name="qkv",
    )(x, g, wt, qg, kg)


def _split_bf16(x):
    hi = x.astype(bf16)
    lo = (x - hi.astype(f32)).astype(bf16)
    return hi, lo


def _attn_prompt_body(nblk, qb_ref, qf_ref, kt_ref, vt_ref, km_ref, bias_ref, o_ref):
    i = pl.program_id(2)
    lane = lax.broadcasted_iota(jnp.int32, (1, 2 * HEAD_DIM), 1)
    row_head = lax.broadcasted_iota(jnp.int32, (2 * HEAD_DIM, MOBA_BLOCK), 0) // HEAD_DIM
    nio = lax.broadcasted_iota(jnp.int32, (nblk, MOBA_BLOCK), 0)
    oh_rows = lax.broadcasted_iota(jnp.int32, (2 * HEAD_DIM, MOBA_BLOCK), 0)
    res = []
    for hh in range(2):
        hmask = (lane // HEAD_DIM) == hh
        qm = jnp.where(hmask, qb_ref[...], jnp.zeros((), bf16))
        q_hi, q_lo = _split_bf16(jnp.where(hmask, qf_ref[...], 0.0))
        k_hi, k_lo = _split_bf16(km_ref[0, 0])
        gate = (lax.dot_general(k_hi, q_hi, _DN_T, preferred_element_type=f32)
                + lax.dot_general(k_lo, q_hi, _DN_T, preferred_element_type=f32)
                + lax.dot_general(k_hi, q_lo, _DN_T, preferred_element_type=f32))[0:nblk]
        rank = jnp.zeros((nblk, MOBA_BLOCK), f32)
        for m in range(nblk):
            gm = gate[m:m + 1, :]
            beats = jnp.where(gm > gate, 1.0, jnp.where(gm == gate, jnp.where(m < nio, 1.0, 0.0), 0.0))
            rank = rank + jnp.where(m < i, beats, 0.0)
        keep = jnp.where(nio < i, jnp.where(rank < MOBA_TOPK, 1.0, 0.0), jnp.where(nio == i, 1.0, 0.0))
        mv_t = jnp.where(keep > 0.5, 0.0, NEG)
        mv = jnp.concatenate([mv_t, jnp.zeros((2 * HEAD_DIM - nblk, MOBA_BLOCK), f32)], axis=0).T
        q_aug = jnp.concatenate([qm, mv.astype(bf16)], axis=1)

        def tile(j, carry, hh=hh, q_aug=q_aug):
            m, acc = carry
            off = pl.multiple_of(j * MOBA_BLOCK, MOBA_BLOCK)
            kt = kt_ref[0, :, pl.ds(off, MOBA_BLOCK)].astype(bf16)
            onehot = jnp.where(oh_rows == j, 1.0, 0.0).astype(bf16)
            s = jnp.dot(q_aug, jnp.concatenate([kt, onehot], axis=0), preferred_element_type=f32)
            s = s + bias_ref[0, hh, jnp.minimum(i - j, 2)]
            m_new = jnp.maximum(m, jnp.max(s, axis=-1, keepdims=True))
            alpha = jnp.exp(m - m_new)
            p = jnp.exp(s - m_new).astype(bf16)
            vt = vt_ref[0, :, pl.ds(off, MOBA_BLOCK)]
            vta = jnp.where(row_head == hh, vt, 1.0).astype(bf16)
            return m_new, alpha * acc + lax.dot_general(p, vta, _DN_T, preferred_element_type=f32)

        m0 = jnp.full((MOBA_BLOCK, 1), -1e38, f32)
        acc0 = jnp.zeros((MOBA_BLOCK, 2 * HEAD_DIM), f32)
        _, acc = lax.fori_loop(0, i + 1, tile, (m0, acc0))
        res.append(acc / pltpu.roll(acc, HEAD_DIM, axis=1))
    o_ref[...] = jnp.where(lane < HEAD_DIM, res[0], res[1]).astype(o_ref.dtype)


def _attn_prompt(qb, qf, kt, vt, km, bias):
    nb, _, tlen = kt.shape
    nq = tlen // MOBA_BLOCK
    return pl.pallas_call(
        functools.partial(_attn_prompt_body, nq),
        grid=(N_PAIRS, nb, nq),
        in_specs=[
            pl.BlockSpec((MOBA_BLOCK, 2 * HEAD_DIM), lambda p, b, i: (b * nq + i, p)),
            pl.BlockSpec((MOBA_BLOCK, 2 * HEAD_DIM), lambda p, b, i: (b * nq + i, p)),
            pl.BlockSpec((1, 2 * HEAD_DIM, tlen), lambda p, b, i: (b, p, 0)),
            pl.BlockSpec((1, 2 * HEAD_DIM, tlen), lambda p, b, i: (b, p, 0)),
            pl.BlockSpec((1, 1, km.shape[2], 2 * HEAD_DIM), lambda p, b, i: (b, p, 0, 0)),
            pl.BlockSpec((1, 2, 3, MOBA_BLOCK, MOBA_BLOCK), lambda p, b, i: (p, 0, 0, 0, 0)),
        ],
        out_specs=pl.BlockSpec((MOBA_BLOCK, 2 * HEAD_DIM), lambda p, b, i: (b * nq + i, p)),
        out_shape=jax.ShapeDtypeStruct((nb * tlen, D_MODEL), bf16),
        compiler_params=_cparams("parallel", "parallel", "arbitrary"),
        name="attn_prompt",
    )(qb, qf, kt, vt, km, bias)


def _attn_sample_body(layer, n_pages, pt_ref, q_ref, kn_ref, vn_ref, bias_ref, ck_hbm, cv_hbm, o_ref,
                      kbuf, vbuf, sems, qt_scr, s_scr, p_scr, ot_scr):
    b = pl.program_id(0)
    nq = q_ref.shape[1]
    npg = n_pages + 1

    def k_copy(p):
        return pltpu.make_async_copy(ck_hbm.at[layer, pt_ref[b, p]], kbuf.at[p], sems.at[0, p])

    def v_copy(p):
        return pltpu.make_async_copy(cv_hbm.at[layer, pt_ref[b, p]], vbuf.at[p], sems.at[1, p])

    for p in range(n_pages):
        k_copy(p).start()
    for p in range(n_pages):
        v_copy(p).start()

    qpad = jnp.concatenate([q_ref[0] * (HEAD_DIM ** -0.5), jnp.zeros((128 - nq, D_MODEL), f32)], axis=0)
    qt_scr[...] = qpad.T
    sub = lax.broadcasted_iota(jnp.int32, (8, PAGE_SIZE), 0)
    lane = lax.broadcasted_iota(jnp.int32, (1, PAGE_SIZE), 1)

    def sublane_total(x):
        x = x + pltpu.roll(x, 4, axis=0)
        x = x + pltpu.roll(x, 2, axis=0)
        return x + pltpu.roll(x, 1, axis=0)

    def scores(kt, qcols):
        out = jnp.zeros((8, PAGE_SIZE), f32)
        for t in range(nq):
            prod = (kt * qcols[t]).reshape(HEAD_DIM // 8, 8, PAGE_SIZE)
            out = out + jnp.where(sub == t, sublane_total(jnp.sum(prod, axis=0)), 0.0)
        return out

    for p in range(n_pages):
        k_copy(p).wait()

    def head_scores(h, c):
        qh = qt_scr[pl.ds(pl.multiple_of(h * HEAD_DIM, HEAD_DIM), HEAD_DIM), :]
        qcols = [jnp.broadcast_to(qh[:, t:t + 1], (HEAD_DIM, PAGE_SIZE)) for t in range(nq)]

        def page(p, c2):
            s_scr[p, h] = scores(kbuf[p, h], qcols)
            return c2

        lax.fori_loop(0, n_pages, page, 0)
        s_scr[n_pages, h] = scores(kn_ref[0, h], qcols)
        return c

    lax.fori_loop(0, N_HEADS, head_scores, 0)

    ppb = MOBA_BLOCK // PAGE_SIZE
    nblk = n_pages // ppb
    gates = []
    for n in range(nblk):
        blk = s_scr[n * ppb]
        for e in range(1, ppb):
            blk = blk + s_scr[n * ppb + e]
        gates.append(jnp.sum(blk, axis=-1, keepdims=True))
    sel = []
    for n in range(nblk):
        rank = jnp.zeros_like(gates[n])
        for m in range(nblk):
            if m == n:
                continue
            if m < n:
                rank = rank + jnp.where(gates[m] >= gates[n], 1.0, 0.0)
            else:
                rank = rank + jnp.where(gates[m] > gates[n], 1.0, 0.0)
        sel.append(rank < MOBA_TOPK)

    mx = jnp.full((N_HEADS, 8, 1), -1e38, f32)
    for p in range(npg):
        s = s_scr[p] + bias_ref[p]
        if p < n_pages:
            s = jnp.where(sel[p // ppb], s, NEG)
        s_scr[p] = s
        mx = jnp.maximum(mx, jnp.max(s, axis=-1, keepdims=True))
    den = jnp.zeros((N_HEADS, 8, 1), f32)
    for p in range(npg):
        e = jnp.exp(s_scr[p] - mx)
        p_scr[p] = e
        den = den + jnp.sum(e, axis=-1, keepdims=True)
    inv = 1.0 / den
    for p in range(npg):
        p_scr[p] = p_scr[p] * inv

    for p in range(n_pages):
        v_copy(p).wait()

    def head_pv(h, c):
        def page(p, accs):
            vt = vbuf[p, h]
            pr = p_scr[p, h]
            return tuple(accs[t] + vt * jnp.broadcast_to(pr[t:t + 1, :], (HEAD_DIM, PAGE_SIZE)) for t in range(nq))

        accs = lax.fori_loop(0, n_pages, page, tuple(jnp.zeros((HEAD_DIM, PAGE_SIZE), f32) for _ in range(nq)))
        vt = vn_ref[0, h]
        pr = p_scr[n_pages, h]
        cols = jnp.zeros((HEAD_DIM, PAGE_SIZE), f32)
        for t in range(nq):
            acc = accs[t] + vt * jnp.broadcast_to(pr[t:t + 1, :], (HEAD_DIM, PAGE_SIZE))
            cols = cols + jnp.where(lane == t, jnp.sum(acc, axis=-1, keepdims=True), 0.0)
        ot_scr[pl.ds(pl.multiple_of(h * HEAD_DIM, HEAD_DIM), HEAD_DIM), :] = cols
        return c

    lax.fori_loop(0, N_HEADS, head_pv, 0)
    o_ref[0] = ot_scr[...].T[0:nq, :].astype(o_ref.dtype)


def _attn_sample(layer, page_table, q, kn, vn, bias, ck, cv):
    nb, nq, _ = q.shape
    n_pages = page_table.shape[1]
    npg = n_pages + 1
    grid_spec = pltpu.PrefetchScalarGridSpec(
        num_scalar_prefetch=1,
        grid=(nb,),
        in_specs=[
            pl.BlockSpec((1, nq, D_MODEL), lambda b, pt: (b, 0, 0)),
            pl.BlockSpec((1, N_HEADS, HEAD_DIM, PAGE_SIZE), lambda b, pt: (b, 0, 0, 0)),
            pl.BlockSpec((1, N_HEADS, HEAD_DIM, PAGE_SIZE), lambda b, pt: (b, 0, 0, 0)),
            pl.BlockSpec((npg, N_HEADS, 8, PAGE_SIZE), lambda b, pt: (0, 0, 0, 0)),
            pl.BlockSpec(memory_space=pl.ANY),
            pl.BlockSpec(memory_space=pl.ANY),
        ],
        out_specs=pl.BlockSpec((1, nq, D_MODEL), lambda b, pt: (b, 0, 0)),
        scratch_shapes=[
            pltpu.VMEM((n_pages, N_HEADS, HEAD_DIM, PAGE_SIZE), f32),
            pltpu.VMEM((n_pages, N_HEADS, HEAD_DIM, PAGE_SIZE), f32),
            pltpu.SemaphoreType.DMA((2, n_pages)),
            pltpu.VMEM((D_MODEL, PAGE_SIZE), f32),
            pltpu.VMEM((npg, N_HEADS, 8, PAGE_SIZE), f32),
            pltpu.VMEM((npg, N_HEADS, 8, PAGE_SIZE), f32),
            pltpu.VMEM((D_MODEL, PAGE_SIZE), f32),
        ],
    )
    return pl.pallas_call(
        functools.partial(_attn_sample_body, layer, n_pages),
        grid_spec=grid_spec,
        out_shape=jax.ShapeDtypeStruct((nb, nq, D_MODEL), f32),
        compiler_params=_cparams("arbitrary"),
        name="attn_sample",
    )(page_table, q, kn, vn, bias, ck, cv)


def _t5_bucket_np(dist):
    dist = np.asarray(dist)
    max_exact = REL_BUCKETS // 2
    log_ratio = (np.log(np.maximum(dist, 1).astype(np.float32) / np.float32(max_exact))
                 / np.float32(math.log(REL_MAX_DIST / max_exact))).astype(np.float32)
    large = np.minimum(max_exact + (log_ratio * np.float32(REL_BUCKETS - max_exact)).astype(np.int32), REL_BUCKETS - 1)
    return np.where(dist < max_exact, dist, large)


def _prompt_bias(rel_bias):
    qi = np.arange(MOBA_BLOCK)[:, None]
    ki = np.arange(MOBA_BLOCK)[None, :]
    tiles = []
    for d in range(3):
        dist = d * MOBA_BLOCK + qi - ki
        b = rel_bias[_t5_bucket_np(np.maximum(dist, 0))]
        tiles.append(jnp.where((dist >= 0)[..., None], b, NEG))
    t = jnp.stack(tiles)
    t = jnp.transpose(t, (3, 0, 1, 2))
    return t.reshape(N_PAIRS, 2, 3, MOBA_BLOCK, MOBA_BLOCK)


def _sample_bias(rel_bias, n_pages, nq):
    past = n_pages * PAGE_SIZE
    kpos = np.arange((n_pages + 1) * PAGE_SIZE)
    qpos = past + np.arange(8)
    dist = qpos[:, None] - kpos[None, :]
    valid = (dist >= 0) & (np.arange(8)[:, None] < nq) & (kpos[None, :] < past + nq)
    b = rel_bias[_t5_bucket_np(np.maximum(dist, 0))]
    b = jnp.where(valid[..., None], b, jnp.where((np.arange(8)[:, None, None] < nq), NEG, 0.0))
    b = jnp.transpose(b, (2, 0, 1)).reshape(N_HEADS, 8, n_pages + 1, PAGE_SIZE)
    return jnp.transpose(b, (2, 0, 1, 3))


def _block_diag(w):
    g, a, b = w.shape
    eye = jnp.eye(g, dtype=w.dtype)
    return jnp.einsum('gab,gh->gahb', w, eye).reshape(g * a, g * b)


def _rec_params(conv_w, conv_b, w_r, b_r, w_i, b_i, lam, lam_re, lam_im, log_step, b_re, b_im, c_re, c_im, d):
    dt = jnp.exp(log_step)[:, None]
    mag = jnp.exp(lam_re * dt)
    ar, ai = mag * jnp.cos(lam_im * dt), mag * jnp.sin(lam_im * dt)
    den = lam_re * lam_re + lam_im * lam_im
    qr = ((ar - 1.0) * lam_re + ai * lam_im) / den
    qi = (ai * lam_re - (ar - 1.0) * lam_im) / den
    bbr = qr[..., None] * b_re - qi[..., None] * b_im
    bbi = qr[..., None] * b_im + qi[..., None] * b_re
    hin, hst = W_S5 // 2, S5_N // 2

    def halves(m):
        return jnp.stack([m[:m.shape[0] // 2, :m.shape[1] // 2], m[m.shape[0] // 2:, m.shape[1] // 2:]]).astype(bf16)

    bbr_m = halves(_block_diag(jnp.transpose(bbr, (0, 2, 1))))
    bbi_m = halves(_block_diag(jnp.transpose(bbi, (0, 2, 1))))
    ccr_m = halves(_block_diag(jnp.transpose(c_re, (0, 2, 1))))
    cci_m = halves(_block_diag(jnp.transpose(c_im, (0, 2, 1))))
    return (conv_w, conv_b[None], _block_diag(w_r).astype(bf16), b_r[None], _block_diag(w_i).astype(bf16), b_i[None],
            lam[None], ar.reshape(1, S5_N), ai.reshape(1, S5_N), bbr_m, bbi_m, ccr_m, cci_m, d[None])


def _to_time_major(x):
    b, t, c = x.shape
    return jnp.transpose(x, (1, 0, 2)).reshape(t * b, c)


def _from_time_major(y, b):
    n, c = y.shape
    return jnp.transpose(y.reshape(n // b, b, c), (1, 0, 2)).reshape(n, c)


def _recurrent_layer(x2d, bsz, tc, g, w_in, conv0, lru0, s5r0, s5i0, prm, glu_w, glu_b, w_out):
    n = x2d.shape[0]
    x_tb = _to_time_major(x2d.reshape(bsz, n // bsz, D_MODEL))
    z = _norm_proj(x_tb, g, w_in)
    y_tb, conv_n, lru_n, s5r_n, s5i_n = _rec_scan(z, bsz, tc, conv0, lru0, s5r0, s5i0, prm)
    y = _from_time_major(y_tb, bsz)
    return _rec_out(x2d, y, glu_w, glu_b, w_out), conv_n, lru_n, s5r_n, s5i_n


def kernel(x_prompt, x_sample, cache_k, cache_v, state_lru_h, state_conv, state_s5_re, state_s5_im, page_table, norm_g, ffn_w_gate, ffn_w_up, ffn_w_down, rec_w_in, rec_conv_w, rec_conv_b, lru_w_r, lru_b_r, lru_w_i, lru_b_i, lru_lambda, s5_lambda_re, s5_lambda_im, s5_log_step, s5_b_re, s5_b_im, s5_c_re, s5_c_im, s5_d, s5_glu_w, s5_glu_b, rec_w_out, att_w_qkv, att_q_norm, att_k_norm, att_w_o, rel_bias):
    bp, tp, _ = x_prompt.shape
    bs, ts, _ = x_sample.shape
    depth = norm_g.shape[0]
    n_pages = page_table.shape[1]
    xp = x_prompt.reshape(bp * tp, D_MODEL)
    xs = x_sample.reshape(bs * ts, D_MODEL)
    ckt = jnp.transpose(cache_k, (0, 1, 3, 4, 2))
    cvt = jnp.transpose(cache_v, (0, 1, 3, 4, 2))
    bias_p = _prompt_bias(rel_bias)
    bias_s = _sample_bias(rel_bias, n_pages, ts)

    k_p, v_p, k_s, v_s = [], [], [], []
    lru_p, lru_s, conv_p, conv_s, s5r_p, s5r_s, s5i_p, s5i_s = [], [], [], [], [], [], [], []
    for l in range(depth):
        for half in (0, 1):
            if half == 1:
                if l % 2 == 0:
                    r = l // 2
                    prm = _rec_params(rec_conv_w[r], rec_conv_b[r], lru_w_r[r], lru_b_r[r], lru_w_i[r], lru_b_i[r],
                                      lru_lambda[r], s5_lambda_re[r], s5_lambda_im[r], s5_log_step[r], s5_b_re[r],
                                      s5_b_im[r], s5_c_re[r], s5_c_im[r], s5_d[r])
                    g = norm_g[l, 1][None]
                    w_in = rec_w_in[r].astype(bf16)
                    glu_w = s5_glu_w[r].astype(bf16)
                    glu_b = s5_glu_b[r][None]
                    w_out = rec_w_out[r].astype(bf16)
                    xp, cp, lp, srp, sip = _recurrent_layer(
                        xp, bp, 64, g, w_in, jnp.zeros((3 * bp, W_LRU), f32), jnp.zeros((bp, W_LRU), f32),
                        jnp.zeros((bp, S5_N), f32), jnp.zeros((bp, S5_N), f32), prm, glu_w, glu_b, w_out)
                    conv0 = jnp.transpose(state_conv[r], (1, 0, 2)).reshape(3 * bs, W_LRU)
                    xs, cs, ls, srs, sis = _recurrent_layer(
                        xs, bs, ts, g, w_in, conv0, state_lru_h[r], state_s5_re[r].reshape(bs, S5_N),
                        state_s5_im[r].reshape(bs, S5_N), prm, glu_w, glu_b, w_out)
                    conv_p.append(jnp.transpose(cp.reshape(3, bp, W_LRU), (1, 0, 2)))
                    conv_s.append(jnp.transpose(cs.reshape(3, bs, W_LRU), (1, 0, 2)))
                    lru_p.append(lp); lru_s.append(ls)
                    s5r_p.append(srp.reshape(bp, S5_GROUPS, S5_STATE)); s5r_s.append(srs.reshape(bs, S5_GROUPS, S5_STATE))
                    s5i_p.append(sip.reshape(bp, S5_GROUPS, S5_STATE)); s5i_s.append(sis.reshape(bs, S5_GROUPS, S5_STATE))
                else:
                    a = l // 2
                    g = norm_g[l, 1][None]
                    wt = att_w_qkv[a].T.astype(bf16)
                    qg = jnp.broadcast_to(att_q_norm[a][:, None], (HEAD_DIM, TM))
                    kg = jnp.broadcast_to(att_k_norm[a][:, None], (HEAD_DIM, TM))
                    w_o = att_w_o[a].astype(bf16)
                    qb, qf, kt, vt, km = _qkv(xp, g, wt, qg, kg, bp)
                    nblk = tp // MOBA_BLOCK
                    km = jnp.transpose(km.reshape(bp, tp // TM, N_PAIRS, 2 * HEAD_DIM, TM // MOBA_BLOCK), (0, 2, 1, 4, 3))
                    km = km.reshape(bp, N_PAIRS, nblk, 2 * HEAD_DIM)
                    km = jnp.pad(km, ((0, 0), (0, 0), (0, 16 - nblk), (0, 0)))
                    op = _attn_prompt(qb, qf, kt, vt, km, bias_p)
                    xp = _att_out(xp, op, w_o)
                    k_p.append(jnp.transpose(kt.reshape(bp, N_HEADS, HEAD_DIM, tp), (0, 3, 1, 2)))
                    v_p.append(jnp.transpose(vt.reshape(bp, N_HEADS, HEAD_DIM, tp), (0, 3, 1, 2)))
                    xs_tb = _to_time_major(xs.reshape(bs, ts, D_MODEL))
                    _, qf_s, kt_s, vt_s, _ = _qkv(xs_tb, g, wt, qg, kg, 1)
                    q_s = jnp.transpose(qf_s.reshape(ts, bs, D_MODEL), (1, 0, 2))
                    kt_s = kt_s.reshape(N_HEADS, HEAD_DIM, ts, bs)
                    vt_s = vt_s.reshape(N_HEADS, HEAD_DIM, ts, bs)
                    kn = jnp.pad(jnp.transpose(kt_s, (3, 0, 1, 2)), ((0, 0), (0, 0), (0, 0), (0, PAGE_SIZE - ts)))
                    vn = jnp.pad(jnp.transpose(vt_s, (3, 0, 1, 2)), ((0, 0), (0, 0), (0, 0), (0, PAGE_SIZE - ts)))
                    os_ = _attn_sample(a, page_table, q_s, kn, vn, bias_s, ckt, cvt)
                    xs = _att_out(xs, os_.reshape(bs * ts, D_MODEL), w_o)
                    k_s.append(jnp.transpose(kt_s, (3, 2, 0, 1)))
                    v_s.append(jnp.transpose(vt_s, (3, 2, 0, 1)))
            else:
                wg, wu, wd = (ffn_w_gate[l, 0].astype(bf16), ffn_w_up[l, 0].astype(bf16), ffn_w_down[l, 0].astype(bf16))
                xp = _ffn(xp, norm_g[l, 0][None], wg, wu, wd)
                xs = _ffn(xs, norm_g[l, 0][None], wg, wu, wd)
        wg, wu, wd = (ffn_w_gate[l, 1].astype(bf16), ffn_w_up[l, 1].astype(bf16), ffn_w_down[l, 1].astype(bf16))
        xp = _ffn(xp, norm_g[l, 2][None], wg, wu, wd)
        xs = _ffn(xs, norm_g[l, 2][None], wg, wu, wd)

    return (xp.reshape(bp, tp, D_MODEL), xs.reshape(bs, ts, D_MODEL),
            jnp.stack(k_p), jnp.stack(v_p), jnp.stack(k_s), jnp.stack(v_s),
            jnp.stack(lru_p), jnp.stack(lru_s), jnp.stack(conv_p), jnp.stack(conv_s),
            jnp.stack(s5r_p), jnp.stack(s5r_s), jnp.stack(s5i_p), jnp.stack(s5i_s))
```

```python
import functools
import math

import numpy as np
import jax
import jax.numpy as jnp
from jax import lax
from jax.experimental import pallas as pl
from jax.experimental.pallas import tpu as pltpu

f32 = jnp.float32
bf16 = jnp.bfloat16

D_MODEL = 1024
D_FF = 2816
W_LRU = 512
W_S5 = 512
LRU_HEADS = 8
LRU_HEAD_DIM = 64
CONV_W = 4
LRU_C = 8.0
S5_GROUP = 16
S5_GROUPS = 32
S5_STATE = 64
S5_N = S5_GROUPS * S5_STATE
N_HEADS = 16
HEAD_DIM = 64
N_PAIRS = N_HEADS // 2
MOBA_BLOCK = 256
MOBA_TOPK = 3
REL_BUCKETS = 32
REL_MAX_DIST = 128
PAGE_SIZE = 128
EPS = 1e-6

TM = 512
ATT_HEAD_GROUP = 4
ATT_ROW_CHUNK = 64
FF_CHUNK = 1408
NEG = -1e30
VMEM_LIMIT = 56 * 1024 * 1024

_DN_T = (((1,), (1,)), ((), ()))


def _cparams(*sem):
    return pltpu.CompilerParams(dimension_semantics=sem, vmem_limit_bytes=VMEM_LIMIT)


def _rms(x, g):
    ms = jnp.mean(x * x, axis=-1, keepdims=True)
    return x * lax.rsqrt(ms + EPS) * g


def _ffn_body(x_ref, g_ref, wg_ref, wu_ref, wd_ref, o_ref, xn_ref):
    k = pl.program_id(1)

    @pl.when(k == 0)
    def _():
        xn_ref[...] = _rms(x_ref[...], g_ref[...]).astype(bf16)

    xn = xn_ref[...]
    gate = jnp.dot(xn, wg_ref[...], preferred_element_type=f32)
    up = jnp.dot(xn, wu_ref[...], preferred_element_type=f32)
    h = (gate * jax.nn.sigmoid(gate) * up).astype(bf16)
    d = 0.5 * jnp.dot(h, wd_ref[...], preferred_element_type=f32)

    @pl.when(k == 0)
    def _():
        o_ref[...] = x_ref[...] + d

    @pl.when(k != 0)
    def _():
        o_ref[...] += d


def _ffn(x, g, wg, wu, wd):
    n = x.shape[0]
    return pl.pallas_call(
        _ffn_body,
        grid=(n // TM, D_FF // FF_CHUNK),
        in_specs=[
            pl.BlockSpec((TM, D_MODEL), lambda i, k: (i, 0)),
            pl.BlockSpec((1, D_MODEL), lambda i, k: (0, 0)),
            pl.BlockSpec((D_MODEL, FF_CHUNK), lambda i, k: (0, k)),
            pl.BlockSpec((D_MODEL, FF_CHUNK), lambda i, k: (0, k)),
            pl.BlockSpec((FF_CHUNK, D_MODEL), lambda i, k: (k, 0)),
        ],
        out_specs=pl.BlockSpec((TM, D_MODEL), lambda i, k: (i, 0)),
        out_shape=jax.ShapeDtypeStruct((n, D_MODEL), f32),
        scratch_shapes=[pltpu.VMEM((TM, D_MODEL), bf16)],
        compiler_params=_cparams("parallel", "arbitrary"),
        name="ffn",
    )(x, g, wg, wu, wd)


def _norm_proj_body(x_ref, g_ref, w_ref, o_ref):
    xn = _rms(x_ref[...], g_ref[...]).astype(bf16)
    o_ref[...] = jnp.dot(xn, w_ref[...], preferred_element_type=f32)


def _norm_proj(x, g, w):
    n, dout = x.shape[0], w.shape[1]
    return pl.pallas_call(
        _norm_proj_body,
        grid=(n // TM,),
        in_specs=[
            pl.BlockSpec((TM, D_MODEL), lambda i: (i, 0)),
            pl.BlockSpec((1, D_MODEL), lambda i: (0, 0)),
            pl.BlockSpec((D_MODEL, dout), lambda i: (0, 0)),
        ],
        out_specs=pl.BlockSpec((TM, dout), lambda i: (i, 0)),
        out_shape=jax.ShapeDtypeStruct((n, dout), f32),
        compiler_params=_cparams("parallel"),
        name="norm_proj",
    )(x, g, w)


def _att_out_body(x_ref, y_ref, w_ref, o_ref):
    o_ref[...] = x_ref[...] + jnp.dot(y_ref[...].astype(bf16), w_ref[...], preferred_element_type=f32)


def _att_out(x, y, w):
    n = x.shape[0]
    return pl.pallas_call(
        _att_out_body,
        grid=(n // TM,),
        in_specs=[
            pl.BlockSpec((TM, D_MODEL), lambda i: (i, 0)),
            pl.BlockSpec((TM, D_MODEL), lambda i: (i, 0)),
            pl.BlockSpec((D_MODEL, D_MODEL), lambda i: (0, 0)),
        ],
        out_specs=pl.BlockSpec((TM, D_MODEL), lambda i: (i, 0)),
        out_shape=jax.ShapeDtypeStruct((n, D_MODEL), f32),
        compiler_params=_cparams("parallel"),
        name="att_out",
    )(x, y, w)


def _rec_out_body(x_ref, y_ref, gw_ref, gb_ref, w_ref, o_ref):
    y = y_ref[...]
    ya = y[:, :W_LRU].astype(bf16)
    g5 = y[:, W_LRU:]
    g5b = g5.astype(bf16)
    glu = jax.nn.sigmoid(jnp.dot(g5b, gw_ref[...], preferred_element_type=f32) + gb_ref[...])
    yb = (g5 * glu).astype(bf16)
    yy = jnp.concatenate([ya, yb], axis=-1)
    o_ref[...] = x_ref[...] + jnp.dot(yy, w_ref[...], preferred_element_type=f32)


def _rec_out(x, y, glu_w, glu_b, w_out):
    n = x.shape[0]
    return pl.pallas_call(
        _rec_out_body,
        grid=(n // TM,),
        in_specs=[
            pl.BlockSpec((TM, D_MODEL), lambda i: (i, 0)),
            pl.BlockSpec((TM, D_MODEL), lambda i: (i, 0)),
            pl.BlockSpec((W_S5, W_S5), lambda i: (0, 0)),
            pl.BlockSpec((1, W_S5), lambda i: (0, 0)),
            pl.BlockSpec((D_MODEL, D_MODEL), lambda i: (0, 0)),
        ],
        out_specs=pl.BlockSpec((TM, D_MODEL), lambda i: (i, 0)),
        out_shape=jax.ShapeDtypeStruct((n, D_MODEL), f32),
        compiler_params=_cparams("parallel"),
        name="rec_out",
    )(x, y, glu_w, glu_b, w_out)


def _rec_scan_body(bsz, tc, z_ref, conv0_ref, lru0_ref, s5r0_ref, s5i0_ref, cw_ref, cb_ref, wr_ref, br_ref,
                   wi_ref, bi_ref, lam_ref, ar_ref, ai_ref, bbr_ref, bbi_ref, ccr_ref, cci_ref, d_ref,
                   y_ref, conv_out, lru_out, s5r_out, s5i_out,
                   xbuf, a_scr, u_scr, hr_scr, hi_scr, lru_c, s5r_c, s5i_c):
    rows = bsz * tc
    step = pl.program_id(0)
    last = pl.num_programs(0) - 1

    @pl.when(step == 0)
    def _():
        xbuf[0:3 * bsz, :] = conv0_ref[...]
        lru_c[...] = lru0_ref[...]
        s5r_c[...] = s5r0_ref[...]
        s5i_c[...] = s5i0_ref[...]

    xbuf[3 * bsz:, :] = z_ref[:, 0:W_LRU]
    xc = cb_ref[...] + cw_ref[0:1, :] * xbuf[0:rows, :]
    for j in range(1, CONV_W):
        xc = xc + cw_ref[j:j + 1, :] * xbuf[j * bsz:j * bsz + rows, :]
    tail = xbuf[rows:rows + 3 * bsz, :]

    xcb = xc.astype(bf16)
    r = jax.nn.sigmoid(jnp.dot(xcb, wr_ref[...], preferred_element_type=f32) + br_ref[...])
    ig = jax.nn.sigmoid(jnp.dot(xcb, wi_ref[...], preferred_element_type=f32) + bi_ref[...])
    lam = lam_ref[...]
    log_sig = jnp.minimum(lam, 0.0) - jnp.log1p(jnp.exp(-jnp.abs(lam)))
    log_a = (LRU_C * r) * log_sig
    a = jnp.exp(log_a)
    mult = jnp.sqrt(-jnp.tanh(log_a) * (a * a + 1.0))
    a_scr[...] = a
    u_scr[...] = mult * (ig * xc)

    def lru_step(t, h):
        rs = pl.ds(pl.multiple_of(t * bsz, 8), bsz)
        h = a_scr[rs, :] * h + u_scr[rs, :]
        u_scr[rs, :] = h
        return h

    h_last = lax.fori_loop(0, tc, lru_step, lru_c[...])
    lru_c[...] = h_last
    y_ref[:, 0:W_LRU] = u_scr[...] * jax.nn.gelu(z_ref[:, W_LRU:2 * W_LRU])

    half_in, half_st = W_S5 // 2, S5_N // 2
    for k in range(2):
        ubk = z_ref[:, 2 * W_LRU + k * half_in:2 * W_LRU + (k + 1) * half_in].astype(bf16)
        hr_scr[:, k * half_st:(k + 1) * half_st] = jnp.dot(ubk, bbr_ref[k], preferred_element_type=f32)
        hi_scr[:, k * half_st:(k + 1) * half_st] = jnp.dot(ubk, bbi_ref[k], preferred_element_type=f32)

    lane_chunk = 512
    for c in range(S5_N // lane_chunk):
        cs = slice(c * lane_chunk, (c + 1) * lane_chunk)
        ar = jnp.broadcast_to(ar_ref[:, cs], (bsz, lane_chunk))
        ai = jnp.broadcast_to(ai_ref[:, cs], (bsz, lane_chunk))

        def s5_step(t, carry, cs=cs, ar=ar, ai=ai):
            hr, hi = carry
            rs = pl.ds(pl.multiple_of(t * bsz, 8), bsz)
            nr = ar * hr - ai * hi + hr_scr[rs, cs]
            ni = ar * hi + ai * hr + hi_scr[rs, cs]
            hr_scr[rs, cs] = nr
            hi_scr[rs, cs] = ni
            return nr, ni

        hr, hi = lax.fori_loop(0, tc, s5_step, (s5r_c[:, cs], s5i_c[:, cs]))
        s5r_c[:, cs] = hr
        s5i_c[:, cs] = hi

    for k in range(2):
        hrk = hr_scr[:, k * half_st:(k + 1) * half_st].astype(bf16)
        hik = hi_scr[:, k * half_st:(k + 1) * half_st].astype(bf16)
        ys = (jnp.dot(hrk, ccr_ref[k], preferred_element_type=f32)
              - jnp.dot(hik, cci_ref[k], preferred_element_type=f32))
        ub = z_ref[:, 2 * W_LRU + k * half_in:2 * W_LRU + (k + 1) * half_in]
        ys = ys + d_ref[:, k * half_in:(k + 1) * half_in] * ub
        y_ref[:, W_LRU + k * half_in:W_LRU + (k + 1) * half_in] = jax.nn.gelu(ys)

    xbuf[0:3 * bsz, :] = tail

    @pl.when(step == last)
    def _():
        conv_out[...] = tail
        lru_out[...] = lru_c[...]
        s5r_out[...] = s5r_c[...]
        s5i_out[...] = s5i_c[...]


def _rec_scan(z, bsz, tc, conv0, lru0, s5r0, s5i0, prm):
    n = z.shape[0]
    rows = bsz * tc
    full = lambda shape: pl.BlockSpec(shape, lambda i: (0,) * len(shape))
    in_specs = [
        pl.BlockSpec((rows, 3 * W_LRU), lambda i: (i, 0)),
        full((3 * bsz, W_LRU)), full((bsz, W_LRU)), full((bsz, S5_N)), full((bsz, S5_N)),
        full((CONV_W, W_LRU)), full((1, W_LRU)),
        full((W_LRU, W_LRU)), full((1, W_LRU)), full((W_LRU, W_LRU)), full((1, W_LRU)), full((1, W_LRU)),
        full((1, S5_N)), full((1, S5_N)),
        full((2, W_S5 // 2, S5_N // 2)), full((2, W_S5 // 2, S5_N // 2)),
        full((2, S5_N // 2, W_S5 // 2)), full((2, S5_N // 2, W_S5 // 2)),
        full((1, W_S5)),
    ]
    out_shape = (
        jax.ShapeDtypeStruct((n, D_MODEL), f32),
        jax.ShapeDtypeStruct((3 * bsz, W_LRU), f32),
        jax.ShapeDtypeStruct((bsz, W_LRU), f32),
        jax.ShapeDtypeStruct((bsz, S5_N), f32),
        jax.ShapeDtypeStruct((bsz, S5_N), f32),
    )
    out_specs = (
        pl.BlockSpec((rows, D_MODEL), lambda i: (i, 0)),
        full((3 * bsz, W_LRU)), full((bsz, W_LRU)), full((bsz, S5_N)), full((bsz, S5_N)),
    )
    scratch = [
        pltpu.VMEM((rows + 3 * bsz, W_LRU), f32),
        pltpu.VMEM((rows, W_LRU), f32), pltpu.VMEM((rows, W_LRU), f32),
        pltpu.VMEM((rows, S5_N), f32), pltpu.VMEM((rows, S5_N), f32),
        pltpu.VMEM((bsz, W_LRU), f32), pltpu.VMEM((bsz, S5_N), f32), pltpu.VMEM((bsz, S5_N), f32),
    ]
    return pl.pallas_call(
        functools.partial(_rec_scan_body, bsz, tc),
        grid=(n // rows,),
        in_specs=in_specs,
        out_specs=out_specs,
        out_shape=out_shape,
        scratch_shapes=scratch,
        compiler_params=_cparams("arbitrary"),
        name="rec_scan",
    )(z, conv0, lru0, s5r0, s5i0, *prm)


def _head_norm_t(t, gain):
    n = t.shape[1]
    t3 = t.reshape(N_HEADS, HEAD_DIM, n)
    ms = jnp.mean(t3 * t3, axis=1, keepdims=True)
    return (t3 * lax.rsqrt(ms + EPS) * gain[None]).reshape(N_HEADS * HEAD_DIM, n)


def _qkv_body(x_ref, g_ref, wt_ref, qg_ref, kg_ref, qb_ref, qf_ref, kt_ref, vt_ref, km_ref):
    xn = _rms(x_ref[...], g_ref[...])
    xnt = xn.T.astype(bf16)
    qkvt = jnp.dot(wt_ref[...], xnt, preferred_element_type=f32)
    qt = _head_norm_t(qkvt[0:D_MODEL], qg_ref[...])
    kt = _head_norm_t(qkvt[D_MODEL:2 * D_MODEL], kg_ref[...])
    q = qt.T
    qf_ref[...] = q
    qb_ref[...] = (q * (HEAD_DIM ** -0.5)).astype(bf16)
    kt_ref[0] = kt
    vt_ref[0] = qkvt[2 * D_MODEL:]
    for n in range(TM // MOBA_BLOCK):
        km_ref[0, :, n:n + 1] = jnp.mean(kt[:, n * MOBA_BLOCK:(n + 1) * MOBA_BLOCK], axis=1, keepdims=True)


def _qkv(x, g, wt, qg, kg, nb):
    n = x.shape[0]
    tlen = n // nb
    tpb = tlen // TM
    return pl.pallas_call(
        _qkv_body,
        grid=(n // TM,),
        in_specs=[
            pl.BlockSpec((TM, D_MODEL), lambda i: (i, 0)),
            pl.BlockSpec((1, D_MODEL), lambda i: (0, 0)),
            pl.BlockSpec((3 * D_MODEL, D_MODEL), lambda i: (0, 0)),
            pl.BlockSpec((HEAD_DIM, TM), lambda i: (0, 0)),
            pl.BlockSpec((HEAD_DIM, TM), lambda i: (0, 0)),
        ],
        out_specs=(
            pl.BlockSpec((TM, D_MODEL), lambda i: (i, 0)),
            pl.BlockSpec((TM, D_MODEL), lambda i: (i, 0)),
            pl.BlockSpec((1, D_MODEL, TM), lambda i: (i // tpb, 0, i % tpb)),
            pl.BlockSpec((1, D_MODEL, TM), lambda i: (i // tpb, 0, i % tpb)),
            pl.BlockSpec((1, D_MODEL, TM // MOBA_BLOCK), lambda i: (i, 0, 0)),
        ),
        out_shape=(
            jax.ShapeDtypeStruct((n, D_MODEL), bf16),
            jax.ShapeDtypeStruct((n, D_MODEL), f32),
            jax.ShapeDtypeStruct((nb, D_MODEL, tlen), f32),
            jax.ShapeDtypeStruct((nb, D_MODEL, tlen), f32),
            jax.ShapeDtypeStruct((n // TM, D_MODEL, TM // MOBA_BLOCK), f32),
        ),
        compiler_params=_cparams("parallel"),
        name="qkv",
    )(x, g, wt, qg, kg)


def _split_bf16(x):
    hi = x.astype(bf16)
    lo = (x - hi.astype(f32)).astype(bf16)
    return hi, lo


def _attn_prompt_body(nblk, hg, qb_ref, qf_ref, kt_ref, vt_ref, km_ref, bias_ref, o_ref,
                      ka_scr, va_scr, qa_scr, s_scr, p_scr):
    i = pl.program_id(2)
    pw = 2 * HEAD_DIM
    tlen = kt_ref.shape[2]
    lane = lax.broadcasted_iota(jnp.int32, (1, pw), 1)
    nio = lax.broadcasted_iota(jnp.int32, (nblk, MOBA_BLOCK), 0)

    @pl.when(i == 0)
    def _():
        blk_rows = lax.broadcasted_iota(jnp.int32, (pw, tlen), 0)
        blk_cols = lax.broadcasted_iota(jnp.int32, (pw, tlen), 1) // MOBA_BLOCK
        onehot = jnp.where((blk_rows == blk_cols) | (blk_rows == blk_cols + nblk), 1.0, 0.0).astype(bf16)
        for pp in range(hg // 2):
            ka_scr[pp, 0:pw, :] = kt_ref[0, pp * pw:(pp + 1) * pw, :].astype(bf16)
            ka_scr[pp, pw:2 * pw, :] = onehot
            va_scr[pp, 0:pw, :] = vt_ref[0, pp * pw:(pp + 1) * pw, :].astype(bf16)
            va_scr[pp, pw:2 * pw, :] = jnp.ones((pw, tlen), bf16)

    for h in range(hg):
        ps = slice((h // 2) * pw, (h // 2 + 1) * pw)
        hmask = (lane // HEAD_DIM) == h % 2
        qa_scr[h, :, 0:pw] = jnp.where(hmask, qb_ref[:, ps], jnp.zeros((), bf16))
        q_hi, q_lo = _split_bf16(jnp.where(hmask, qf_ref[:, ps], 0.0))
        k_hi, k_lo = _split_bf16(km_ref[0, 0, :, ps])
        gate = (lax.dot_general(k_hi, q_hi, _DN_T, preferred_element_type=f32)
                + lax.dot_general(k_lo, q_hi, _DN_T, preferred_element_type=f32)
                + lax.dot_general(k_hi, q_lo, _DN_T, preferred_element_type=f32))[0:nblk]
        rank = jnp.zeros((nblk, MOBA_BLOCK), f32)
        for m in range(nblk):
            gm = gate[m:m + 1, :]
            beats = jnp.where(gm > gate, 1.0, jnp.where(gm == gate, jnp.where(m < nio, 1.0, 0.0), 0.0))
            rank = rank + jnp.where(m < i, beats, 0.0)
        keep = jnp.where(nio < i, jnp.where(rank < MOBA_TOPK, 1.0, 0.0), jnp.where(nio == i, 1.0, 0.0)) > 0.5
        far = nio < i - 1
        b_far = bias_ref[0, h, 2, 0:1, 0:1]
        b_hi = b_far.astype(bf16).astype(f32)
        term_hi = jnp.where(keep, jnp.where(far, b_hi, 0.0), NEG)
        term_lo = jnp.where(keep & far, b_far - b_hi, 0.0)
        terms = jnp.concatenate([term_hi, term_lo, jnp.zeros((pw - 2 * nblk, MOBA_BLOCK), f32)], axis=0).T
        qa_scr[h, :, pw:2 * pw] = terms.astype(bf16)

    near = pl.multiple_of(i * MOBA_BLOCK, MOBA_BLOCK)
    prev = pl.multiple_of(jnp.maximum(i - 1, 0) * MOBA_BLOCK, MOBA_BLOCK)
    prev_tile = jnp.where(i >= 1, 1, 3)
    nrow = 2 * MOBA_BLOCK

    def process(width):
        for pp in range(hg // 2):
            qa = qa_scr[2 * pp:2 * pp + 2].reshape(nrow, 2 * pw)
            s_scr[pp, :, 0:width] = jnp.dot(qa, ka_scr[pp, :, 0:width], preferred_element_type=f32)
        for h in range(hg):
            rows = slice((h % 2) * MOBA_BLOCK, (h % 2 + 1) * MOBA_BLOCK)
            s_scr[h // 2, rows, pl.ds(near, MOBA_BLOCK)] += bias_ref[0, h, 0]
            s_scr[h // 2, rows, pl.ds(prev, MOBA_BLOCK)] += bias_ref[0, h, prev_tile]
        for pp in range(hg // 2):
            for r in range(nrow // ATT_ROW_CHUNK):
                rs = slice(r * ATT_ROW_CHUNK, (r + 1) * ATT_ROW_CHUNK)
                mx = s_scr[pp, rs, 0:MOBA_BLOCK]
                for t in range(1, width // MOBA_BLOCK):
                    mx = jnp.maximum(mx, s_scr[pp, rs, t * MOBA_BLOCK:(t + 1) * MOBA_BLOCK])
                mx = jnp.max(mx, axis=-1, keepdims=True)
                for t in range(width // MOBA_BLOCK):
                    cs = slice(t * MOBA_BLOCK, (t + 1) * MOBA_BLOCK)
                    p_scr[pp, rs, cs] = jnp.exp(s_scr[pp, rs, cs] - mx).astype(bf16)
        for pp in range(hg // 2):
            o = lax.dot_general(p_scr[pp, :, 0:width], va_scr[pp, :, 0:width], _DN_T, preferred_element_type=f32)
            res = [o[sub * MOBA_BLOCK:(sub + 1) * MOBA_BLOCK, 0:pw] / o[sub * MOBA_BLOCK:(sub + 1) * MOBA_BLOCK, pw:2 * pw]
                   for sub in range(2)]
            o_ref[:, pp * pw:(pp + 1) * pw] = jnp.where(lane < HEAD_DIM, res[0], res[1]).astype(o_ref.dtype)

    for k in range(nblk // 2):
        @pl.when(i // 2 == k)
        def _(k=k):
            process((2 * k + 2) * MOBA_BLOCK)


def _attn_prompt(qb, qf, kt, vt, km, bias):
    nb, _, tlen = kt.shape
    nq = tlen // MOBA_BLOCK
    hg = ATT_HEAD_GROUP
    gw = hg * HEAD_DIM
    return pl.pallas_call(
        functools.partial(_attn_prompt_body, nq, hg),
        grid=(N_HEADS // hg, nb, nq),
        in_specs=[
            pl.BlockSpec((MOBA_BLOCK, gw), lambda p, b, i: (b * nq + i, p)),
            pl.BlockSpec((MOBA_BLOCK, gw), lambda p, b, i: (b * nq + i, p)),
            pl.BlockSpec((1, gw, tlen), lambda p, b, i: (b, p, 0)),
            pl.BlockSpec((1, gw, tlen), lambda p, b, i: (b, p, 0)),
            pl.BlockSpec((1, 1, km.shape[2], gw), lambda p, b, i: (b, p, 0, 0)),
            pl.BlockSpec((1, hg, 4, MOBA_BLOCK, MOBA_BLOCK), lambda p, b, i: (p, 0, 0, 0, 0)),
        ],
        out_specs=pl.BlockSpec((MOBA_BLOCK, gw), lambda p, b, i: (b * nq + i, p)),
        out_shape=jax.ShapeDtypeStruct((nb * tlen, D_MODEL), bf16),
        scratch_shapes=[
            pltpu.VMEM((hg // 2, 4 * HEAD_DIM, tlen), bf16),
            pltpu.VMEM((hg // 2, 4 * HEAD_DIM, tlen), bf16),
            pltpu.VMEM((hg, MOBA_BLOCK, 4 * HEAD_DIM), bf16),
            pltpu.VMEM((hg // 2, 2 * MOBA_BLOCK, tlen), f32),
            pltpu.VMEM((hg // 2, 2 * MOBA_BLOCK, tlen), bf16),
        ],
        compiler_params=_cparams("parallel", "parallel", "arbitrary"),
        name="attn_prompt",
    )(qb, qf, kt, vt, km, bias)


def _attn_sample_body(layer, n_pages, pt_ref, q_ref, kn_ref, vn_ref, bias_ref, ck_hbm, cv_hbm, o_ref,
                      kbuf, vbuf, sems, s_scr, p_scr):
    b = pl.program_id(0)
    slot = b % 2
    nq = q_ref.shape[1]
    npg = n_pages + 1
    rows = N_HEADS * 8

    def k_copy(bb, sl, p):
        return pltpu.make_async_copy(ck_hbm.at[layer, pt_ref[bb, p]], kbuf.at[sl, p], sems.at[0, sl, p])

    def v_copy(bb, sl, p):
        return pltpu.make_async_copy(cv_hbm.at[layer, pt_ref[bb, p]], vbuf.at[sl, p], sems.at[1, sl, p])

    def start_all(bb, sl):
        for p in range(n_pages):
            k_copy(bb, sl, p).start()
        for p in range(n_pages):
            v_copy(bb, sl, p).start()

    @pl.when(b == 0)
    def _():
        start_all(0, 0)

    @pl.when(b + 1 < pl.num_programs(0))
    def _():
        start_all(b + 1, 1 - slot)

    q8 = jnp.concatenate([q_ref[0] * (HEAD_DIM ** -0.5), jnp.zeros((8 - nq, D_MODEL), f32)], axis=0)
    rh = lax.broadcasted_iota(jnp.int32, (rows, D_MODEL), 0) // 8
    ch = lax.broadcasted_iota(jnp.int32, (rows, D_MODEL), 1) // HEAD_DIM
    qblk = jnp.where(rh == ch, jnp.tile(q8, (N_HEADS, 1)), 0.0).astype(bf16)

    def page_scores(kpage):
        kp = kpage.reshape(N_HEADS * HEAD_DIM, PAGE_SIZE).astype(bf16)
        return jnp.dot(qblk, kp, preferred_element_type=f32).reshape(N_HEADS, 8, PAGE_SIZE)

    for p in range(n_pages):
        k_copy(b, slot, p).wait()
    for p in range(n_pages):
        s_scr[p] = page_scores(kbuf[slot, p])
    def new_page(ref):
        return jnp.concatenate([ref[0], jnp.zeros((PAGE_SIZE - nq, D_MODEL), f32)], axis=0).astype(bf16)

    s_scr[n_pages] = lax.dot_general(qblk, new_page(kn_ref), _DN_T,
                                     preferred_element_type=f32).reshape(N_HEADS, 8, PAGE_SIZE)

    ppb = MOBA_BLOCK // PAGE_SIZE
    nblk = n_pages // ppb
    gates = []
    for n in range(nblk):
        blk = s_scr[n * ppb]
        for e in range(1, ppb):
            blk = blk + s_scr[n * ppb + e]
        gates.append(jnp.sum(blk, axis=-1, keepdims=True))
    sel = []
    for n in range(nblk):
        rank = jnp.zeros_like(gates[n])
        for m in range(nblk):
            if m == n:
                continue
            if m < n:
                rank = rank + jnp.where(gates[m] >= gates[n], 1.0, 0.0)
            else:
                rank = rank + jnp.where(gates[m] > gates[n], 1.0, 0.0)
        sel.append(rank < MOBA_TOPK)

    mx_e = jnp.full((N_HEADS, 8, PAGE_SIZE), -1e38, f32)
    for p in range(npg):
        s = s_scr[p] + bias_ref[p]
        if p < n_pages:
            s = jnp.where(sel[p // ppb], s, NEG)
        s_scr[p] = s
        mx_e = jnp.maximum(mx_e, s)
    mx = jnp.max(mx_e, axis=-1, keepdims=True)
    den_e = jnp.zeros((N_HEADS, 8, PAGE_SIZE), f32)
    for p in range(npg):
        e = jnp.exp(s_scr[p] - mx)
        p_scr[p] = e
        den_e = den_e + e
    inv = 1.0 / jnp.sum(den_e, axis=-1, keepdims=True)

    for p in range(n_pages):
        v_copy(b, slot, p).wait()

    def page_pv(p, vpage):
        pr = (p_scr[p] * inv).reshape(rows, PAGE_SIZE).astype(bf16)
        vp = vpage.reshape(N_HEADS * HEAD_DIM, PAGE_SIZE).astype(bf16)
        return lax.dot_general(pr, vp, _DN_T, preferred_element_type=f32)

    pr_new = (p_scr[n_pages] * inv).reshape(rows, PAGE_SIZE).astype(bf16)
    acc = jnp.dot(pr_new, new_page(vn_ref), preferred_element_type=f32)
    for p in range(n_pages):
        acc = acc + page_pv(p, vbuf[slot, p])
    lane_head = lax.broadcasted_iota(jnp.int32, (8, D_MODEL), 1) // HEAD_DIM
    out = jnp.zeros((8, D_MODEL), f32)
    for h in range(N_HEADS):
        out = out + jnp.where(lane_head == h, acc[h * 8:(h + 1) * 8, :], 0.0)
    o_ref[0] = out[0:nq, :].astype(o_ref.dtype)


def _attn_sample(layer, page_table, q, kn, vn, bias, ck, cv):
    nb, nq, _ = q.shape
    n_pages = page_table.shape[1]
    npg = n_pages + 1
    grid_spec = pltpu.PrefetchScalarGridSpec(
        num_scalar_prefetch=1,
        grid=(nb,),
        in_specs=[
            pl.BlockSpec((1, nq, D_MODEL), lambda b, pt: (b, 0, 0)),
            pl.BlockSpec((1, nq, D_MODEL), lambda b, pt: (b, 0, 0)),
            pl.BlockSpec((1, nq, D_MODEL), lambda b, pt: (b, 0, 0)),
            pl.BlockSpec((npg, N_HEADS, 8, PAGE_SIZE), lambda b, pt: (0, 0, 0, 0)),
            pl.BlockSpec(memory_space=pl.ANY),
            pl.BlockSpec(memory_space=pl.ANY),
        ],
        out_specs=pl.BlockSpec((1, nq, D_MODEL), lambda b, pt: (b, 0, 0)),
        scratch_shapes=[
            pltpu.VMEM((2, n_pages, N_HEADS, HEAD_DIM, PAGE_SIZE), f32),
            pltpu.VMEM((2, n_pages, N_HEADS, HEAD_DIM, PAGE_SIZE), f32),
            pltpu.SemaphoreType.DMA((2, 2, n_pages)),
            pltpu.VMEM((npg, N_HEADS, 8, PAGE_SIZE), f32),
            pltpu.VMEM((npg, N_HEADS, 8, PAGE_SIZE), f32),
        ],
    )
    return pl.pallas_call(
        functools.partial(_attn_sample_body, layer, n_pages),
        grid_spec=grid_spec,
        out_shape=jax.ShapeDtypeStruct((nb, nq, D_MODEL), f32),
        compiler_params=_cparams("arbitrary"),
        name="attn_sample",
    )(page_table, q, kn, vn, bias, ck, cv)


def _t5_bucket_np(dist):
    dist = np.asarray(dist)
    max_exact = REL_BUCKETS // 2
    log_ratio = (np.log(np.maximum(dist, 1).astype(np.float32) / np.float32(max_exact))
                 / np.float32(math.log(REL_MAX_DIST / max_exact))).astype(np.float32)
    large = np.minimum(max_exact + (log_ratio * np.float32(REL_BUCKETS - max_exact)).astype(np.int32), REL_BUCKETS - 1)
    return np.where(dist < max_exact, dist, large)


def _prompt_bias(rel_bias):
    blk, span = MOBA_BLOCK, 2 * MOBA_BLOCK
    by_dist = rel_bias[_t5_bucket_np(np.arange(3 * blk))].T
    j = np.arange(span)
    j = np.where(j < blk, j, j - span)
    tiles = []
    for d in range(3):
        dist = d * blk - j
        w = jnp.where((dist >= 0)[None, :], by_dist[:, np.clip(dist, 0, 3 * blk - 1)], NEG)
        t = jnp.tile(w, (1, blk))[:, :blk * (span - 1)].reshape(N_HEADS, blk, span - 1)
        tiles.append(t[:, :, :blk])
    tiles.append(jnp.zeros_like(tiles[0]))
    t = jnp.stack(tiles, axis=1)
    return t.reshape(N_HEADS // ATT_HEAD_GROUP, ATT_HEAD_GROUP, 4, blk, blk)


def _group_block_means(km, nb, tlen):
    gw = ATT_HEAD_GROUP * HEAD_DIM
    nblk = tlen // MOBA_BLOCK
    km = jnp.transpose(km.reshape(nb, tlen // TM, N_HEADS // ATT_HEAD_GROUP, gw, TM // MOBA_BLOCK), (0, 2, 1, 4, 3))
    km = km.reshape(nb, N_HEADS // ATT_HEAD_GROUP, nblk, gw)
    return jnp.pad(km, ((0, 0), (0, 0), (0, 16 - nblk), (0, 0)))


def _sample_bias(rel_bias, n_pages, nq):
    past = n_pages * PAGE_SIZE
    kpos = np.arange((n_pages + 1) * PAGE_SIZE)
    qpos = past + np.arange(8)
    dist = qpos[:, None] - kpos[None, :]
    valid = (dist >= 0) & (np.arange(8)[:, None] < nq) & (kpos[None, :] < past + nq)
    b = rel_bias[_t5_bucket_np(np.maximum(dist, 0))]
    b = jnp.where(valid[..., None], b, jnp.where((np.arange(8)[:, None, None] < nq), NEG, 0.0))
    b = jnp.transpose(b, (2, 0, 1)).reshape(N_HEADS, 8, n_pages + 1, PAGE_SIZE)
    return jnp.transpose(b, (2, 0, 1, 3))


def _block_diag(w):
    g, a, b = w.shape
    eye = jnp.eye(g, dtype=w.dtype)
    return jnp.einsum('gab,gh->gahb', w, eye).reshape(g * a, g * b)


def _rec_params(conv_w, conv_b, w_r, b_r, w_i, b_i, lam, lam_re, lam_im, log_step, b_re, b_im, c_re, c_im, d):
    dt = jnp.exp(log_step)[:, None]
    mag = jnp.exp(lam_re * dt)
    ar, ai = mag * jnp.cos(lam_im * dt), mag * jnp.sin(lam_im * dt)
    den = lam_re * lam_re + lam_im * lam_im
    qr = ((ar - 1.0) * lam_re + ai * lam_im) / den
    qi = (ai * lam_re - (ar - 1.0) * lam_im) / den
    bbr = qr[..., None] * b_re - qi[..., None] * b_im
    bbi = qr[..., None] * b_im + qi[..., None] * b_re
    hin, hst = W_S5 // 2, S5_N // 2

    def halves(m):
        return jnp.stack([m[:m.shape[0] // 2, :m.shape[1] // 2], m[m.shape[0] // 2:, m.shape[1] // 2:]]).astype(bf16)

    bbr_m = halves(_block_diag(jnp.transpose(bbr, (0, 2, 1))))
    bbi_m = halves(_block_diag(jnp.transpose(bbi, (0, 2, 1))))
    ccr_m = halves(_block_diag(jnp.transpose(c_re, (0, 2, 1))))
    cci_m = halves(_block_diag(jnp.transpose(c_im, (0, 2, 1))))
    return (conv_w, conv_b[None], _block_diag(w_r).astype(bf16), b_r[None], _block_diag(w_i).astype(bf16), b_i[None],
            lam[None], ar.reshape(1, S5_N), ai.reshape(1, S5_N), bbr_m, bbi_m, ccr_m, cci_m, d[None])


def _to_time_major(x):
    b, t, c = x.shape
    return jnp.transpose(x, (1, 0, 2)).reshape(t * b, c)


def _from_time_major(y, b):
    n, c = y.shape
    return jnp.transpose(y.reshape(n // b, b, c), (1, 0, 2)).reshape(n, c)


def _recurrent_layer(x2d, bsz, tc, g, w_in, conv0, lru0, s5r0, s5i0, prm, glu_w, glu_b, w_out):
    n = x2d.shape[0]
    x_tb = _to_time_major(x2d.reshape(bsz, n // bsz, D_MODEL))
    z = _norm_proj(x_tb, g, w_in)
    y_tb, conv_n, lru_n, s5r_n, s5i_n = _rec_scan(z, bsz, tc, conv0, lru0, s5r0, s5i0, prm)
    y = _from_time_major(y_tb, bsz)
    return _rec_out(x2d, y, glu_w, glu_b, w_out), conv_n, lru_n, s5r_n, s5i_n


def kernel(x_prompt, x_sample, cache_k, cache_v, state_lru_h, state_conv, state_s5_re, state_s5_im, page_table, norm_g, ffn_w_gate, ffn_w_up, ffn_w_down, rec_w_in, rec_conv_w, rec_conv_b, lru_w_r, lru_b_r, lru_w_i, lru_b_i, lru_lambda, s5_lambda_re, s5_lambda_im, s5_log_step, s5_b_re, s5_b_im, s5_c_re, s5_c_im, s5_d, s5_glu_w, s5_glu_b, rec_w_out, att_w_qkv, att_q_norm, att_k_norm, att_w_o, rel_bias):
    bp, tp, _ = x_prompt.shape
    bs, ts, _ = x_sample.shape
    depth = norm_g.shape[0]
    n_pages = page_table.shape[1]
    xp = x_prompt.reshape(bp * tp, D_MODEL)
    xs = x_sample.reshape(bs * ts, D_MODEL)
    ckt = jnp.transpose(cache_k, (0, 1, 3, 4, 2))
    cvt = jnp.transpose(cache_v, (0, 1, 3, 4, 2))
    bias_p = _prompt_bias(rel_bias)
    bias_s = _sample_bias(rel_bias, n_pages, ts)

    k_p, v_p, k_s, v_s = [], [], [], []
    lru_p, lru_s, conv_p, conv_s, s5r_p, s5r_s, s5i_p, s5i_s = [], [], [], [], [], [], [], []
    for l in range(depth):
        for half in (0, 1):
            if half == 1:
                if l % 2 == 0:
                    r = l // 2
                    prm = _rec_params(rec_conv_w[r], rec_conv_b[r], lru_w_r[r], lru_b_r[r], lru_w_i[r], lru_b_i[r],
                                      lru_lambda[r], s5_lambda_re[r], s5_lambda_im[r], s5_log_step[r], s5_b_re[r],
                                      s5_b_im[r], s5_c_re[r], s5_c_im[r], s5_d[r])
                    g = norm_g[l, 1][None]
                    w_in = rec_w_in[r].astype(bf16)
                    glu_w = s5_glu_w[r].astype(bf16)
                    glu_b = s5_glu_b[r][None]
                    w_out = rec_w_out[r].astype(bf16)
                    xp, cp, lp, srp, sip = _recurrent_layer(
                        xp, bp, 64, g, w_in, jnp.zeros((3 * bp, W_LRU), f32), jnp.zeros((bp, W_LRU), f32),
                        jnp.zeros((bp, S5_N), f32), jnp.zeros((bp, S5_N), f32), prm, glu_w, glu_b, w_out)
                    conv0 = jnp.transpose(state_conv[r], (1, 0, 2)).reshape(3 * bs, W_LRU)
                    xs, cs, ls, srs, sis = _recurrent_layer(
                        xs, bs, ts, g, w_in, conv0, state_lru_h[r], state_s5_re[r].reshape(bs, S5_N),
                        state_s5_im[r].reshape(bs, S5_N), prm, glu_w, glu_b, w_out)
                    conv_p.append(jnp.transpose(cp.reshape(3, bp, W_LRU), (1, 0, 2)))
                    conv_s.append(jnp.transpose(cs.reshape(3, bs, W_LRU), (1, 0, 2)))
                    lru_p.append(lp); lru_s.append(ls)
                    s5r_p.append(srp.reshape(bp, S5_GROUPS, S5_STATE)); s5r_s.append(srs.reshape(bs, S5_GROUPS, S5_STATE))
                    s5i_p.append(sip.reshape(bp, S5_GROUPS, S5_STATE)); s5i_s.append(sis.reshape(bs, S5_GROUPS, S5_STATE))
                else:
                    a = l // 2
                    g = norm_g[l, 1][None]
                    wt = att_w_qkv[a].T.astype(bf16)
                    qg = jnp.broadcast_to(att_q_norm[a][:, None], (HEAD_DIM, TM))
                    kg = jnp.broadcast_to(att_k_norm[a][:, None], (HEAD_DIM, TM))
                    w_o = att_w_o[a].astype(bf16)
                    qb, qf, kt, vt, km = _qkv(xp, g, wt, qg, kg, bp)
                    nblk = tp // MOBA_BLOCK
                    km = _group_block_means(km, bp, tp)
                    op = _attn_prompt(qb, qf, kt, vt, km, bias_p)
                    xp = _att_out(xp, op, w_o)
                    k_p.append(jnp.transpose(kt.reshape(bp, N_HEADS, HEAD_DIM, tp), (0, 3, 1, 2)))
                    v_p.append(jnp.transpose(vt.reshape(bp, N_HEADS, HEAD_DIM, tp), (0, 3, 1, 2)))
                    xs_tb = _to_time_major(xs.reshape(bs, ts, D_MODEL))
                    _, qf_s, kt_s, vt_s, _ = _qkv(xs_tb, g, wt, qg, kg, 1)
                    q_s = jnp.transpose(qf_s.reshape(ts, bs, D_MODEL), (1, 0, 2))
                    kt_s = kt_s.reshape(N_HEADS, HEAD_DIM, ts, bs)
                    vt_s = vt_s.reshape(N_HEADS, HEAD_DIM, ts, bs)
                    k_s.append(jnp.transpose(kt_s, (3, 2, 0, 1)))
                    v_s.append(jnp.transpose(vt_s, (3, 2, 0, 1)))
                    os_ = _attn_sample(a, page_table, q_s, k_s[-1].reshape(bs, ts, D_MODEL),
                                       v_s[-1].reshape(bs, ts, D_MODEL), bias_s, ckt, cvt)
                    xs = _att_out(xs, os_.reshape(bs * ts, D_MODEL), w_o)
            else:
                wg, wu, wd = (ffn_w_gate[l, 0].astype(bf16), ffn_w_up[l, 0].astype(bf16), ffn_w_down[l, 0].astype(bf16))
                xp = _ffn(xp, norm_g[l, 0][None], wg, wu, wd)
                xs = _ffn(xs, norm_g[l, 0][None], wg, wu, wd)
        wg, wu, wd = (ffn_w_gate[l, 1].astype(bf16), ffn_w_up[l, 1].astype(bf16), ffn_w_down[l, 1].astype(bf16))
        xp = _ffn(xp, norm_g[l, 2][None], wg, wu, wd)
        xs = _ffn(xs, norm_g[l, 2][None], wg, wu, wd)

    return (xp.reshape(bp, tp, D_MODEL), xs.reshape(bs, ts, D_MODEL),
            jnp.stack(k_p), jnp.stack(v_p), jnp.stack(k_s), jnp.stack(v_s),
            jnp.stack(lru_p), jnp.stack(lru_s), jnp.stack(conv_p), jnp.stack(conv_s),
            jnp.stack(s5r_p), jnp.stack(s5r_s), jnp.stack(s5i_p), jnp.stack(s5i_s))
```

```python
import functools
import math

import numpy as np
import jax
import jax.numpy as jnp
from jax import lax
from jax.experimental import pallas as pl
from jax.experimental.pallas import tpu as pltpu

f32 = jnp.float32
bf16 = jnp.bfloat16

D_MODEL = 1024
D_FF = 2816
W_LRU = 512
W_S5 = 512
CONV_W = 4
LRU_C = 8.0
S5_GROUPS = 32
S5_STATE = 64
S5_N = S5_GROUPS * S5_STATE
N_HEADS = 16
HEAD_DIM = 64
MOBA_BLOCK = 256
MOBA_TOPK = 3
REL_BUCKETS = 32
REL_MAX_DIST = 128
PAGE_SIZE = 128
EPS = 1e-6

TM = 512
FF_CHUNK = 256
REC_TIME_CHUNK = 64
ATT_HEAD_GROUP = 4
ATT_Q_BLOCKS = 2
ATT_ROW_CHUNK = 64
NEG = -1e30
VMEM_LIMIT = 56 * 1024 * 1024

_DN_T = (((1,), (1,)), ((), ()))


def _cparams(*sem):
    return pltpu.CompilerParams(dimension_semantics=sem, vmem_limit_bytes=VMEM_LIMIT)


def _layer_spec(shape, idx):
    return pl.BlockSpec((None,) + shape, lambda *_: (idx,) + (0,) * len(shape))


def _rms(x, g):
    ms = jnp.mean(x * x, axis=-1, keepdims=True)
    return x * lax.rsqrt(ms + EPS) * g


def _ffn_body(x_ref, g_ref, wg_ref, wu_ref, wd_ref, o_ref):
    x = x_ref[...]
    xn = _rms(x, g_ref[...]).astype(bf16)
    d = None
    for c in range(D_FF // FF_CHUNK):
        cs = slice(c * FF_CHUNK, (c + 1) * FF_CHUNK)
        gate = jnp.dot(xn, wg_ref[:, cs], preferred_element_type=f32)
        up = jnp.dot(xn, wu_ref[:, cs], preferred_element_type=f32)
        h = (gate * jax.nn.sigmoid(gate) * up).astype(bf16)
        dc = jnp.dot(h, wd_ref[cs, :], preferred_element_type=f32)
        d = dc if d is None else d + dc
    o_ref[...] = x + 0.5 * d


def _ffn(x, g, wg, wu, wd, l, s):
    n = x.shape[0]
    return pl.pallas_call(
        _ffn_body,
        grid=(n // TM,),
        in_specs=[
            pl.BlockSpec((TM, D_MODEL), lambda i: (i, 0)),
            _layer_spec((1, D_MODEL), 3 * l + 2 * s),
            _layer_spec((D_MODEL, D_FF), 2 * l + s),
            _layer_spec((D_MODEL, D_FF), 2 * l + s),
            _layer_spec((D_FF, D_MODEL), 2 * l + s),
        ],
        out_specs=pl.BlockSpec((TM, D_MODEL), lambda i: (i, 0)),
        out_shape=jax.ShapeDtypeStruct((n, D_MODEL), f32),
        compiler_params=_cparams("parallel"),
        name="ffn",
    )(x, g, wg, wu, wd)


def _norm_proj_body(x_ref, g_ref, w_ref, o_ref):
    xn = _rms(x_ref[...], g_ref[...]).astype(bf16)
    o_ref[...] = jnp.dot(xn, w_ref[...], preferred_element_type=f32)


def _norm_proj(x, g, gi, w, wi):
    n, dout = x.shape[0], w.shape[2]
    return pl.pallas_call(
        _norm_proj_body,
        grid=(n // TM,),
        in_specs=[
            pl.BlockSpec((TM, D_MODEL), lambda i: (i, 0)),
            _layer_spec((1, D_MODEL), gi),
            _layer_spec((D_MODEL, dout), wi),
        ],
        out_specs=pl.BlockSpec((TM, dout), lambda i: (i, 0)),
        out_shape=jax.ShapeDtypeStruct((n, dout), f32),
        compiler_params=_cparams("parallel"),
        name="norm_proj",
    )(x, g, w)


def _att_out_body(x_ref, y_ref, w_ref, o_ref):
    o_ref[...] = x_ref[...] + jnp.dot(y_ref[...].astype(bf16), w_ref[...], preferred_element_type=f32)


def _att_out(x, y, w, a):
    n = x.shape[0]
    return pl.pallas_call(
        _att_out_body,
        grid=(n // TM,),
        in_specs=[
            pl.BlockSpec((TM, D_MODEL), lambda i: (i, 0)),
            pl.BlockSpec((TM, D_MODEL), lambda i: (i, 0)),
            _layer_spec((D_MODEL, D_MODEL), a),
        ],
        out_specs=pl.BlockSpec((TM, D_MODEL), lambda i: (i, 0)),
        out_shape=jax.ShapeDtypeStruct((n, D_MODEL), f32),
        compiler_params=_cparams("parallel"),
        name="att_out",
    )(x, y, w)


def _rec_out_body(x_ref, y_ref, gw_ref, gb_ref, w_ref, o_ref):
    y = y_ref[...]
    ya = y[:, :W_LRU].astype(bf16)
    g5 = y[:, W_LRU:]
    glu = jax.nn.sigmoid(jnp.dot(g5.astype(bf16), gw_ref[...], preferred_element_type=f32) + gb_ref[...])
    yb = (g5 * glu).astype(bf16)
    yy = jnp.concatenate([ya, yb], axis=-1)
    o_ref[...] = x_ref[...] + jnp.dot(yy, w_ref[...], preferred_element_type=f32)


def _rec_out(x, y, glu_w, glu_b, w_out, r):
    n = x.shape[0]
    return pl.pallas_call(
        _rec_out_body,
        grid=(n // TM,),
        in_specs=[
            pl.BlockSpec((TM, D_MODEL), lambda i: (i, 0)),
            pl.BlockSpec((TM, D_MODEL), lambda i: (i, 0)),
            _layer_spec((W_S5, W_S5), r),
            _layer_spec((1, W_S5), r),
            _layer_spec((D_MODEL, D_MODEL), r),
        ],
        out_specs=pl.BlockSpec((TM, D_MODEL), lambda i: (i, 0)),
        out_shape=jax.ShapeDtypeStruct((n, D_MODEL), f32),
        compiler_params=_cparams("parallel"),
        name="rec_out",
    )(x, y, glu_w, glu_b, w_out)


def _rec_scan_body(bsz, tc, z_ref, conv0_ref, lru0_ref, s5r0_ref, s5i0_ref, cw_ref, cb_ref, wr_ref, br_ref,
                   wi_ref, bi_ref, lam_ref, ar_ref, ai_ref, bbr_ref, bbi_ref, ccr_ref, cci_ref, d_ref,
                   y_ref, conv_out, lru_out, s5r_out, s5i_out,
                   xbuf, a_scr, u_scr, hr_scr, hi_scr, lru_c, s5r_c, s5i_c):
    rows = bsz * tc
    step = pl.program_id(0)
    last = pl.num_programs(0) - 1

    @pl.when(step == 0)
    def _():
        xbuf[0:3 * bsz, :] = conv0_ref[...]
        lru_c[...] = lru0_ref[...]
        s5r_c[...] = s5r0_ref[...]
        s5i_c[...] = s5i0_ref[...]

    xbuf[3 * bsz:, :] = z_ref[:, 0:W_LRU]
    xc = cb_ref[...] + cw_ref[0:1, :] * xbuf[0:rows, :]
    for j in range(1, CONV_W):
        xc = xc + cw_ref[j:j + 1, :] * xbuf[j * bsz:j * bsz + rows, :]
    tail = xbuf[rows:rows + 3 * bsz, :]

    xcb = xc.astype(bf16)
    r = jax.nn.sigmoid(jnp.dot(xcb, wr_ref[...], preferred_element_type=f32) + br_ref[...])
    ig = jax.nn.sigmoid(jnp.dot(xcb, wi_ref[...], preferred_element_type=f32) + bi_ref[...])
    lam = lam_ref[...]
    log_sig = jnp.minimum(lam, 0.0) - jnp.log1p(jnp.exp(-jnp.abs(lam)))
    log_a = (LRU_C * r) * log_sig
    a = jnp.exp(log_a)
    mult = jnp.sqrt(-jnp.tanh(log_a) * (a * a + 1.0))
    a_scr[...] = a
    u_scr[...] = mult * (ig * xc)

    def lru_step(t, h):
        rs = pl.ds(pl.multiple_of(t * bsz, 8), bsz)
        h = a_scr[rs, :] * h + u_scr[rs, :]
        u_scr[rs, :] = h
        return h

    h_last = lax.fori_loop(0, tc, lru_step, lru_c[...])
    lru_c[...] = h_last
    y_ref[:, 0:W_LRU] = u_scr[...] * jax.nn.gelu(z_ref[:, W_LRU:2 * W_LRU])

    half_in, half_st = W_S5 // 2, S5_N // 2
    for k in range(2):
        ubk = z_ref[:, 2 * W_LRU + k * half_in:2 * W_LRU + (k + 1) * half_in].astype(bf16)
        hr_scr[:, k * half_st:(k + 1) * half_st] = jnp.dot(ubk, bbr_ref[k], preferred_element_type=f32)
        hi_scr[:, k * half_st:(k + 1) * half_st] = jnp.dot(ubk, bbi_ref[k], preferred_element_type=f32)

    lane_chunk = 512
    for c in range(S5_N // lane_chunk):
        cs = slice(c * lane_chunk, (c + 1) * lane_chunk)
        ar = jnp.broadcast_to(ar_ref[:, cs], (bsz, lane_chunk))
        ai = jnp.broadcast_to(ai_ref[:, cs], (bsz, lane_chunk))

        def s5_step(t, carry, cs=cs, ar=ar, ai=ai):
            hr, hi = carry
            rs = pl.ds(pl.multiple_of(t * bsz, 8), bsz)
            nr = ar * hr - ai * hi + hr_scr[rs, cs]
            ni = ar * hi + ai * hr + hi_scr[rs, cs]
            hr_scr[rs, cs] = nr
            hi_scr[rs, cs] = ni
            return nr, ni

        hr, hi = lax.fori_loop(0, tc, s5_step, (s5r_c[:, cs], s5i_c[:, cs]))
        s5r_c[:, cs] = hr
        s5i_c[:, cs] = hi

    for k in range(2):
        hrk = hr_scr[:, k * half_st:(k + 1) * half_st].astype(bf16)
        hik = hi_scr[:, k * half_st:(k + 1) * half_st].astype(bf16)
        ys = (jnp.dot(hrk, ccr_ref[k], preferred_element_type=f32)
              - jnp.dot(hik, cci_ref[k], preferred_element_type=f32))
        ub = z_ref[:, 2 * W_LRU + k * half_in:2 * W_LRU + (k + 1) * half_in]
        ys = ys + d_ref[:, k * half_in:(k + 1) * half_in] * ub
        y_ref[:, W_LRU + k * half_in:W_LRU + (k + 1) * half_in] = jax.nn.gelu(ys)

    xbuf[0:3 * bsz, :] = tail

    @pl.when(step == last)
    def _():
        conv_out[...] = tail
        lru_out[...] = lru_c[...]
        s5r_out[...] = s5r_c[...]
        s5i_out[...] = s5i_c[...]


def _rec_scan(z, bsz, tc, conv0, lru0, s5r0, s5i0, prm, r):
    n = z.shape[0]
    rows = bsz * tc
    full = lambda shape: pl.BlockSpec(shape, lambda i: (0,) * len(shape))
    in_specs = [
        pl.BlockSpec((rows, 3 * W_LRU), lambda i: (i, 0)),
        full((3 * bsz, W_LRU)), full((bsz, W_LRU)), full((bsz, S5_N)), full((bsz, S5_N)),
    ] + [_layer_spec(p.shape[1:], r) for p in prm]
    out_shape = (
        jax.ShapeDtypeStruct((n, D_MODEL), f32),
        jax.ShapeDtypeStruct((3 * bsz, W_LRU), f32),
        jax.ShapeDtypeStruct((bsz, W_LRU), f32),
        jax.ShapeDtypeStruct((bsz, S5_N), f32),
        jax.ShapeDtypeStruct((bsz, S5_N), f32),
    )
    out_specs = (
        pl.BlockSpec((rows, D_MODEL), lambda i: (i, 0)),
        full((3 * bsz, W_LRU)), full((bsz, W_LRU)), full((bsz, S5_N)), full((bsz, S5_N)),
    )
    scratch = [
        pltpu.VMEM((rows + 3 * bsz, W_LRU), f32),
        pltpu.VMEM((rows, W_LRU), f32), pltpu.VMEM((rows, W_LRU), f32),
        pltpu.VMEM((rows, S5_N), f32), pltpu.VMEM((rows, S5_N), f32),
        pltpu.VMEM((bsz, W_LRU), f32), pltpu.VMEM((bsz, S5_N), f32), pltpu.VMEM((bsz, S5_N), f32),
    ]
    return pl.pallas_call(
        functools.partial(_rec_scan_body, bsz, tc),
        grid=(n // rows,),
        in_specs=in_specs,
        out_specs=out_specs,
        out_shape=out_shape,
        scratch_shapes=scratch,
        compiler_params=_cparams("arbitrary"),
        name="rec_scan",
    )(z, conv0, lru0, s5r0, s5i0, *prm)


def _head_norm_t(t, gain):
    n = t.shape[1]
    t3 = t.reshape(N_HEADS, HEAD_DIM, n)
    ms = jnp.mean(t3 * t3, axis=1, keepdims=True)
    return (t3 * lax.rsqrt(ms + EPS) * gain[None]).reshape(N_HEADS * HEAD_DIM, n)


def _qkv_body(n_alias, x_ref, g_ref, wt_ref, qg_ref, kg_ref, *refs):
    qb_ref, qf_ref, kt_ref, vt_ref, km_ref = refs[n_alias:]
    xn = _rms(x_ref[...], g_ref[...])
    xnt = xn.T.astype(bf16)
    qkvt = jnp.dot(wt_ref[...], xnt, preferred_element_type=f32)
    qt = _head_norm_t(qkvt[0:D_MODEL], qg_ref[...])
    kt = _head_norm_t(qkvt[D_MODEL:2 * D_MODEL], kg_ref[...])
    q = qt.T
    qf_ref[...] = q
    qb_ref[...] = (q * (HEAD_DIM ** -0.5)).astype(bf16)
    kt_ref[0] = kt
    vt_ref[0] = qkvt[2 * D_MODEL:]
    for n in range(TM // MOBA_BLOCK):
        km_ref[0, :, n:n + 1] = jnp.mean(kt[:, n * MOBA_BLOCK:(n + 1) * MOBA_BLOCK], axis=1, keepdims=True)


def _qkv(x, g, gi, wt, qg, kg, a, nb, kv_prev=None):
    n = x.shape[0]
    tlen = n // nb
    tpb = tlen // TM
    n_att = wt.shape[0]
    kv_spec = pl.BlockSpec((None, 1, D_MODEL, TM), lambda i: (a, i // tpb, 0, i % tpb))
    prev = () if kv_prev is None else tuple(kv_prev)
    return pl.pallas_call(
        functools.partial(_qkv_body, len(prev)),
        grid=(n // TM,),
        in_specs=[
            pl.BlockSpec((TM, D_MODEL), lambda i: (i, 0)),
            _layer_spec((1, D_MODEL), gi),
            _layer_spec((3 * D_MODEL, D_MODEL), a),
            _layer_spec((HEAD_DIM, TM), a),
            _layer_spec((HEAD_DIM, TM), a),
        ] + [pl.BlockSpec(memory_space=pl.ANY)] * len(prev),
        out_specs=(
            pl.BlockSpec((TM, D_MODEL), lambda i: (i, 0)),
            pl.BlockSpec((TM, D_MODEL), lambda i: (i, 0)),
            kv_spec,
            kv_spec,
            pl.BlockSpec((1, D_MODEL, TM // MOBA_BLOCK), lambda i: (i, 0, 0)),
        ),
        out_shape=(
            jax.ShapeDtypeStruct((n, D_MODEL), bf16),
            jax.ShapeDtypeStruct((n, D_MODEL), f32),
            jax.ShapeDtypeStruct((n_att, nb, D_MODEL, tlen), f32),
            jax.ShapeDtypeStruct((n_att, nb, D_MODEL, tlen), f32),
            jax.ShapeDtypeStruct((n // TM, D_MODEL, TM // MOBA_BLOCK), f32),
        ),
        input_output_aliases={5 + j: 2 + j for j in range(len(prev))},
        compiler_params=_cparams("parallel"),
        name="qkv",
    )(x, g, wt, qg, kg, *prev)


def _split_bf16(x):
    hi = x.astype(bf16)
    lo = (x - hi.astype(f32)).astype(bf16)
    return hi, lo


def _attn_prompt_body(nblk, hg, nqb, qb_ref, qf_ref, kt_ref, vt_ref, km_ref, bias_ref, o_ref,
                      ka_scr, va_scr, qa_scr, s_scr, p_scr):
    step = pl.program_id(2)
    pw = 2 * HEAD_DIM
    tlen = kt_ref.shape[2]
    qrows = nqb * MOBA_BLOCK
    lane = lax.broadcasted_iota(jnp.int32, (1, pw), 1)
    nio = lax.broadcasted_iota(jnp.int32, (nblk, MOBA_BLOCK), 0)

    @pl.when(step == 0)
    def _():
        blk_rows = lax.broadcasted_iota(jnp.int32, (pw, tlen), 0)
        blk_cols = lax.broadcasted_iota(jnp.int32, (pw, tlen), 1) // MOBA_BLOCK
        onehot = jnp.where((blk_rows == blk_cols) | (blk_rows == blk_cols + nblk), 1.0, 0.0).astype(bf16)
        for pp in range(hg // 2):
            ka_scr[pp, 0:pw, :] = kt_ref[0, pp * pw:(pp + 1) * pw, :].astype(bf16)
            ka_scr[pp, pw:2 * pw, :] = onehot
            va_scr[pp, 0:pw, :] = vt_ref[0, pp * pw:(pp + 1) * pw, :].astype(bf16)
            va_scr[pp, pw:2 * pw, :] = jnp.ones((pw, tlen), bf16)

    for h in range(hg):
        ps = slice((h // 2) * pw, (h // 2 + 1) * pw)
        hmask = (lane // HEAD_DIM) == h % 2
        k_hi, k_lo = _split_bf16(km_ref[0, 0, :, ps])
        b_far = bias_ref[0, h, 2, 0:1, 0:1]
        b_hi = b_far.astype(bf16).astype(f32)
        for qi in range(nqb):
            i = step * nqb + qi
            rows = slice(qi * MOBA_BLOCK, (qi + 1) * MOBA_BLOCK)
            qa_scr[h, rows, 0:pw] = jnp.where(hmask, qb_ref[rows, ps], jnp.zeros((), bf16))
            q_hi, q_lo = _split_bf16(jnp.where(hmask, qf_ref[rows, ps], 0.0))
            gate = (lax.dot_general(k_hi, q_hi, _DN_T, preferred_element_type=f32)
                    + lax.dot_general(k_lo, q_hi, _DN_T, preferred_element_type=f32)
                    + lax.dot_general(k_hi, q_lo, _DN_T, preferred_element_type=f32))[0:nblk]
            rank = jnp.zeros((nblk, MOBA_BLOCK), f32)
            for m in range(nblk):
                gm = gate[m:m + 1, :]
                beats = jnp.where(gm > gate, 1.0, jnp.where(gm == gate, jnp.where(m < nio, 1.0, 0.0), 0.0))
                rank = rank + jnp.where(m < i, beats, 0.0)
            keep = jnp.where(nio < i, jnp.where(rank < MOBA_TOPK, 1.0, 0.0), jnp.where(nio == i, 1.0, 0.0)) > 0.5
            far = nio < i - 1
            term_hi = jnp.where(keep, jnp.where(far, b_hi, 0.0), NEG)
            term_lo = jnp.where(keep & far, b_far - b_hi, 0.0)
            terms = jnp.concatenate([term_hi, term_lo, jnp.zeros((pw - 2 * nblk, MOBA_BLOCK), f32)], axis=0).T
            qa_scr[h, rows, pw:2 * pw] = terms.astype(bf16)

    nrow = 2 * qrows

    def process(width):
        for pp in range(hg // 2):
            qa = qa_scr[2 * pp:2 * pp + 2].reshape(nrow, 2 * pw)
            s_scr[pp, :, 0:width] = jnp.dot(qa, ka_scr[pp, :, 0:width], preferred_element_type=f32)
        for h in range(hg):
            for qi in range(nqb):
                i = step * nqb + qi
                rows = slice((h % 2) * qrows + qi * MOBA_BLOCK, (h % 2) * qrows + (qi + 1) * MOBA_BLOCK)
                near = pl.multiple_of(i * MOBA_BLOCK, MOBA_BLOCK)
                prev = pl.multiple_of(jnp.maximum(i - 1, 0) * MOBA_BLOCK, MOBA_BLOCK)
                prev_tile = jnp.where(i >= 1, 1, 3)
                s_scr[h // 2, rows, pl.ds(near, MOBA_BLOCK)] += bias_ref[0, h, 0]
                s_scr[h // 2, rows, pl.ds(prev, MOBA_BLOCK)] += bias_ref[0, h, prev_tile]
        for pp in range(hg // 2):
            for r in range(nrow // ATT_ROW_CHUNK):
                rs = slice(r * ATT_ROW_CHUNK, (r + 1) * ATT_ROW_CHUNK)
                mx = s_scr[pp, rs, 0:MOBA_BLOCK]
                for t in range(1, width // MOBA_BLOCK):
                    mx = jnp.maximum(mx, s_scr[pp, rs, t * MOBA_BLOCK:(t + 1) * MOBA_BLOCK])
                mx = jnp.max(mx, axis=-1, keepdims=True)
                for t in range(width // MOBA_BLOCK):
                    cs = slice(t * MOBA_BLOCK, (t + 1) * MOBA_BLOCK)
                    p_scr[pp, rs, cs] = jnp.exp(s_scr[pp, rs, cs] - mx).astype(bf16)
        for pp in range(hg // 2):
            o = lax.dot_general(p_scr[pp, :, 0:width], va_scr[pp, :, 0:width], _DN_T, preferred_element_type=f32)
            res = [o[sub * qrows:(sub + 1) * qrows, 0:pw] / o[sub * qrows:(sub + 1) * qrows, pw:2 * pw]
                   for sub in range(2)]
            o_ref[:, pp * pw:(pp + 1) * pw] = jnp.where(lane < HEAD_DIM, res[0], res[1]).astype(o_ref.dtype)

    for k in range(nblk // nqb):
        @pl.when(step == k)
        def _(k=k):
            process((k + 1) * qrows)


def _attn_prompt(qb, qf, kt, vt, a, km, bias):
    _, nb, _, tlen = kt.shape
    nq = tlen // MOBA_BLOCK
    hg, nqb = ATT_HEAD_GROUP, ATT_Q_BLOCKS
    gw = hg * HEAD_DIM
    qrows = nqb * MOBA_BLOCK
    steps = nq // nqb
    kv_spec = pl.BlockSpec((None, 1, gw, tlen), lambda p, b, i: (a, b, p, 0))
    return pl.pallas_call(
        functools.partial(_attn_prompt_body, nq, hg, nqb),
        grid=(N_HEADS // hg, nb, steps),
        in_specs=[
            pl.BlockSpec((qrows, gw), lambda p, b, i: (b * steps + i, p)),
            pl.BlockSpec((qrows, gw), lambda p, b, i: (b * steps + i, p)),
            kv_spec,
            kv_spec,
            pl.BlockSpec((1, 1, km.shape[2], gw), lambda p, b, i: (b, p, 0, 0)),
            pl.BlockSpec((1, hg, 4, MOBA_BLOCK, MOBA_BLOCK), lambda p, b, i: (p, 0, 0, 0, 0)),
        ],
        out_specs=pl.BlockSpec((qrows, gw), lambda p, b, i: (b * steps + i, p)),
        out_shape=jax.ShapeDtypeStruct((nb * tlen, D_MODEL), bf16),
        scratch_shapes=[
            pltpu.VMEM((hg // 2, 4 * HEAD_DIM, tlen), bf16),
            pltpu.VMEM((hg // 2, 4 * HEAD_DIM, tlen), bf16),
            pltpu.VMEM((hg, qrows, 4 * HEAD_DIM), bf16),
            pltpu.VMEM((hg // 2, 2 * qrows, tlen), f32),
            pltpu.VMEM((hg // 2, 2 * qrows, tlen), bf16),
        ],
        compiler_params=_cparams("parallel", "parallel", "arbitrary"),
        name="attn_prompt",
    )(qb, qf, kt, vt, km, bias)


def _attn_sample_body(layer, n_pages, pt_ref, q_ref, kn_ref, vn_ref, bias_ref, ck_hbm, cv_hbm, o_ref,
                      kbuf, vbuf, sems, s_scr, p_scr):
    b = pl.program_id(0)
    slot = b % 2
    nq = q_ref.shape[1]
    npg = n_pages + 1
    rows = N_HEADS * 8

    def k_copy(bb, sl, p):
        return pltpu.make_async_copy(ck_hbm.at[layer, pt_ref[bb, p]], kbuf.at[sl, p], sems.at[0, sl, p])

    def v_copy(bb, sl, p):
        return pltpu.make_async_copy(cv_hbm.at[layer, pt_ref[bb, p]], vbuf.at[sl, p], sems.at[1, sl, p])

    def start_all(bb, sl):
        for p in range(n_pages):
            k_copy(bb, sl, p).start()
        for p in range(n_pages):
            v_copy(bb, sl, p).start()

    @pl.when(b == 0)
    def _():
        start_all(0, 0)

    @pl.when(b + 1 < pl.num_programs(0))
    def _():
        start_all(b + 1, 1 - slot)

    q8 = jnp.concatenate([q_ref[0] * (HEAD_DIM ** -0.5), jnp.zeros((8 - nq, D_MODEL), f32)], axis=0)
    rh = lax.broadcasted_iota(jnp.int32, (rows, D_MODEL), 0) // 8
    ch = lax.broadcasted_iota(jnp.int32, (rows, D_MODEL), 1) // HEAD_DIM
    qblk = jnp.where(rh == ch, jnp.tile(q8, (N_HEADS, 1)), 0.0).astype(bf16)

    def page_scores(kpage):
        kp = kpage.reshape(N_HEADS * HEAD_DIM, PAGE_SIZE).astype(bf16)
        return jnp.dot(qblk, kp, preferred_element_type=f32).reshape(N_HEADS, 8, PAGE_SIZE)

    for p in range(n_pages):
        k_copy(b, slot, p).wait()
    for p in range(n_pages):
        s_scr[p] = page_scores(kbuf[slot, p])

    def new_page(ref):
        return jnp.concatenate([ref[0], jnp.zeros((PAGE_SIZE - nq, D_MODEL), f32)], axis=0).astype(bf16)

    s_scr[n_pages] = lax.dot_general(qblk, new_page(kn_ref), _DN_T,
                                     preferred_element_type=f32).reshape(N_HEADS, 8, PAGE_SIZE)

    ppb = MOBA_BLOCK // PAGE_SIZE
    nblk = n_pages // ppb
    gates = []
    for n in range(nblk):
        blk = s_scr[n * ppb]
        for e in range(1, ppb):
            blk = blk + s_scr[n * ppb + e]
        gates.append(jnp.sum(blk, axis=-1, keepdims=True))
    sel = []
    for n in range(nblk):
        rank = jnp.zeros_like(gates[n])
        for m in range(nblk):
            if m == n:
                continue
            if m < n:
                rank = rank + jnp.where(gates[m] >= gates[n], 1.0, 0.0)
            else:
                rank = rank + jnp.where(gates[m] > gates[n], 1.0, 0.0)
        sel.append(rank < MOBA_TOPK)

    mx_e = jnp.full((N_HEADS, 8, PAGE_SIZE), -1e38, f32)
    for p in range(npg):
        s = s_scr[p] + bias_ref[p]
        if p < n_pages:
            s = jnp.where(sel[p // ppb], s, NEG)
        s_scr[p] = s
        mx_e = jnp.maximum(mx_e, s)
    mx = jnp.max(mx_e, axis=-1, keepdims=True)
    den_e = jnp.zeros((N_HEADS, 8, PAGE_SIZE), f32)
    for p in range(npg):
        e = jnp.exp(s_scr[p] - mx)
        p_scr[p] = e
        den_e = den_e + e
    inv = 1.0 / jnp.sum(den_e, axis=-1, keepdims=True)

    for p in range(n_pages):
        v_copy(b, slot, p).wait()

    def page_pv(p, vpage):
        pr = (p_scr[p] * inv).reshape(rows, PAGE_SIZE).astype(bf16)
        vp = vpage.reshape(N_HEADS * HEAD_DIM, PAGE_SIZE).astype(bf16)
        return lax.dot_general(pr, vp, _DN_T, preferred_element_type=f32)

    pr_new = (p_scr[n_pages] * inv).reshape(rows, PAGE_SIZE).astype(bf16)
    acc = jnp.dot(pr_new, new_page(vn_ref), preferred_element_type=f32)
    for p in range(n_pages):
        acc = acc + page_pv(p, vbuf[slot, p])
    lane_head = lax.broadcasted_iota(jnp.int32, (8, D_MODEL), 1) // HEAD_DIM
    out = jnp.zeros((8, D_MODEL), f32)
    for h in range(N_HEADS):
        out = out + jnp.where(lane_head == h, acc[h * 8:(h + 1) * 8, :], 0.0)
    o_ref[0] = out[0:nq, :].astype(o_ref.dtype)


def _attn_sample(layer, page_table, q, kn, vn, bias, ck, cv):
    nb, nq, _ = q.shape
    n_pages = page_table.shape[1]
    npg = n_pages + 1
    grid_spec = pltpu.PrefetchScalarGridSpec(
        num_scalar_prefetch=1,
        grid=(nb,),
        in_specs=[
            pl.BlockSpec((1, nq, D_MODEL), lambda b, pt: (b, 0, 0)),
            pl.BlockSpec((1, nq, D_MODEL), lambda b, pt: (b, 0, 0)),
            pl.BlockSpec((1, nq, D_MODEL), lambda b, pt: (b, 0, 0)),
            pl.BlockSpec((npg, N_HEADS, 8, PAGE_SIZE), lambda b, pt: (0, 0, 0, 0)),
            pl.BlockSpec(memory_space=pl.ANY),
            pl.BlockSpec(memory_space=pl.ANY),
        ],
        out_specs=pl.BlockSpec((1, nq, D_MODEL), lambda b, pt: (b, 0, 0)),
        scratch_shapes=[
            pltpu.VMEM((2, n_pages, N_HEADS, HEAD_DIM, PAGE_SIZE), f32),
            pltpu.VMEM((2, n_pages, N_HEADS, HEAD_DIM, PAGE_SIZE), f32),
            pltpu.SemaphoreType.DMA((2, 2, n_pages)),
            pltpu.VMEM((npg, N_HEADS, 8, PAGE_SIZE), f32),
            pltpu.VMEM((npg, N_HEADS, 8, PAGE_SIZE), f32),
        ],
    )
    return pl.pallas_call(
        functools.partial(_attn_sample_body, layer, n_pages),
        grid_spec=grid_spec,
        out_shape=jax.ShapeDtypeStruct((nb, nq, D_MODEL), f32),
        compiler_params=_cparams("arbitrary"),
        name="attn_sample",
    )(page_table, q, kn, vn, bias, ck, cv)


def _t5_bucket_np(dist):
    dist = np.asarray(dist)
    max_exact = REL_BUCKETS // 2
    log_ratio = (np.log(np.maximum(dist, 1).astype(np.float32) / np.float32(max_exact))
                 / np.float32(math.log(REL_MAX_DIST / max_exact))).astype(np.float32)
    large = np.minimum(max_exact + (log_ratio * np.float32(REL_BUCKETS - max_exact)).astype(np.int32), REL_BUCKETS - 1)
    return np.where(dist < max_exact, dist, large)


def _prompt_bias(rel_bias):
    blk, span = MOBA_BLOCK, 2 * MOBA_BLOCK
    by_dist = rel_bias[_t5_bucket_np(np.arange(3 * blk))].T
    j = np.arange(span)
    j = np.where(j < blk, j, j - span)
    tiles = []
    for d in range(3):
        dist = d * blk - j
        w = jnp.where((dist >= 0)[None, :], by_dist[:, np.clip(dist, 0, 3 * blk - 1)], NEG)
        t = jnp.tile(w, (1, blk))[:, :blk * (span - 1)].reshape(N_HEADS, blk, span - 1)
        tiles.append(t[:, :, :blk])
    tiles.append(jnp.zeros_like(tiles[0]))
    t = jnp.stack(tiles, axis=1)
    return t.reshape(N_HEADS // ATT_HEAD_GROUP, ATT_HEAD_GROUP, 4, blk, blk)


def _group_block_means(km, nb, tlen):
    gw = ATT_HEAD_GROUP * HEAD_DIM
    nblk = tlen // MOBA_BLOCK
    km = jnp.transpose(km.reshape(nb, tlen // TM, N_HEADS // ATT_HEAD_GROUP, gw, TM // MOBA_BLOCK), (0, 2, 1, 4, 3))
    km = km.reshape(nb, N_HEADS // ATT_HEAD_GROUP, nblk, gw)
    return jnp.pad(km, ((0, 0), (0, 0), (0, 16 - nblk), (0, 0)))


def _sample_bias(rel_bias, n_pages, nq):
    past = n_pages * PAGE_SIZE
    nkeys = (n_pages + 1) * PAGE_SIZE
    span = past + 8
    by_dist = rel_bias[_t5_bucket_np(np.arange(span))].T
    rev = jnp.pad(by_dist[:, ::-1], ((0, 0), (0, nkeys)))
    rows = jnp.stack([rev[:, 7 - t:7 - t + nkeys] for t in range(8)], axis=1)
    kpos = np.arange(nkeys)
    t8 = np.arange(8)[:, None]
    valid = (past + t8 - kpos[None, :] >= 0) & (t8 < nq) & (kpos[None, :] < past + nq)
    fill = np.where(t8 < nq, NEG, 0.0).astype(np.float32)
    b = jnp.where(valid[None], rows, fill[None])
    return jnp.transpose(b.reshape(N_HEADS, 8, n_pages + 1, PAGE_SIZE), (2, 0, 1, 3))


def _block_diag(w):
    g, a, b = w.shape
    eye = jnp.eye(g, dtype=w.dtype)
    return jnp.einsum('gab,gh->gahb', w, eye).reshape(g * a, g * b)


def _rec_params(conv_w, conv_b, w_r, b_r, w_i, b_i, lam, lam_re, lam_im, log_step, b_re, b_im, c_re, c_im, d):
    dt = jnp.exp(log_step)[:, None]
    mag = jnp.exp(lam_re * dt)
    ar, ai = mag * jnp.cos(lam_im * dt), mag * jnp.sin(lam_im * dt)
    den = lam_re * lam_re + lam_im * lam_im
    qr = ((ar - 1.0) * lam_re + ai * lam_im) / den
    qi = (ai * lam_re - (ar - 1.0) * lam_im) / den
    bbr = qr[..., None] * b_re - qi[..., None] * b_im
    bbi = qr[..., None] * b_im + qi[..., None] * b_re

    def halves(m):
        return jnp.stack([m[:m.shape[0] // 2, :m.shape[1] // 2], m[m.shape[0] // 2:, m.shape[1] // 2:]]).astype(bf16)

    bbr_m = halves(_block_diag(jnp.transpose(bbr, (0, 2, 1))))
    bbi_m = halves(_block_diag(jnp.transpose(bbi, (0, 2, 1))))
    ccr_m = halves(_block_diag(jnp.transpose(c_re, (0, 2, 1))))
    cci_m = halves(_block_diag(jnp.transpose(c_im, (0, 2, 1))))
    return (conv_w, conv_b[None], _block_diag(w_r).astype(bf16), b_r[None], _block_diag(w_i).astype(bf16), b_i[None],
            lam[None], ar.reshape(1, S5_N), ai.reshape(1, S5_N), bbr_m, bbi_m, ccr_m, cci_m, d[None])


def _to_time_major(x):
    b, t, c = x.shape
    return jnp.transpose(x, (1, 0, 2)).reshape(t * b, c)


def _from_time_major(y, b):
    n, c = y.shape
    return jnp.transpose(y.reshape(n // b, b, c), (1, 0, 2)).reshape(n, c)


def _recurrent_layer(x2d, bsz, tc, g, gi, w_in, conv0, lru0, s5r0, s5i0, prm, glu_w, glu_b, w_out, r):
    n = x2d.shape[0]
    x_tb = _to_time_major(x2d.reshape(bsz, n // bsz, D_MODEL))
    z = _norm_proj(x_tb, g, gi, w_in, r)
    y_tb, conv_n, lru_n, s5r_n, s5i_n = _rec_scan(z, bsz, tc, conv0, lru0, s5r0, s5i0, prm, r)
    y = _from_time_major(y_tb, bsz)
    return _rec_out(x2d, y, glu_w, glu_b, w_out, r), conv_n, lru_n, s5r_n, s5i_n


def kernel(x_prompt, x_sample, cache_k, cache_v, state_lru_h, state_conv, state_s5_re, state_s5_im, page_table, norm_g, ffn_w_gate, ffn_w_up, ffn_w_down, rec_w_in, rec_conv_w, rec_conv_b, lru_w_r, lru_b_r, lru_w_i, lru_b_i, lru_lambda, s5_lambda_re, s5_lambda_im, s5_log_step, s5_b_re, s5_b_im, s5_c_re, s5_c_im, s5_d, s5_glu_w, s5_glu_b, rec_w_out, att_w_qkv, att_q_norm, att_k_norm, att_w_o, rel_bias):
    bp, tp, _ = x_prompt.shape
    bs, ts, _ = x_sample.shape
    depth = norm_g.shape[0]
    n_pages = page_table.shape[1]
    xp = x_prompt.reshape(bp * tp, D_MODEL)
    xs = x_sample.reshape(bs * ts, D_MODEL)

    g_all = norm_g.reshape(depth * 3, 1, D_MODEL)
    wg_all = ffn_w_gate.astype(bf16).reshape(depth * 2, D_MODEL, D_FF)
    wu_all = ffn_w_up.astype(bf16).reshape(depth * 2, D_MODEL, D_FF)
    wd_all = ffn_w_down.astype(bf16).reshape(depth * 2, D_FF, D_MODEL)
    w_in_all = rec_w_in.astype(bf16)
    glu_w_all = s5_glu_w.astype(bf16)
    glu_b_all = s5_glu_b[:, None, :]
    w_out_all = rec_w_out.astype(bf16)
    rec_prm = jax.vmap(_rec_params)(rec_conv_w, rec_conv_b, lru_w_r, lru_b_r, lru_w_i, lru_b_i, lru_lambda,
                                    s5_lambda_re, s5_lambda_im, s5_log_step, s5_b_re, s5_b_im, s5_c_re, s5_c_im, s5_d)
    wt_all = jnp.transpose(att_w_qkv, (0, 2, 1)).astype(bf16)
    qg_all = jnp.broadcast_to(att_q_norm[:, :, None], att_q_norm.shape + (TM,))
    kg_all = jnp.broadcast_to(att_k_norm[:, :, None], att_k_norm.shape + (TM,))
    w_o_all = att_w_o.astype(bf16)
    ckt = jnp.transpose(cache_k, (0, 1, 3, 4, 2))
    cvt = jnp.transpose(cache_v, (0, 1, 3, 4, 2))
    bias_p = _prompt_bias(rel_bias)
    bias_s = _sample_bias(rel_bias, n_pages, ts)

    kv_p = kv_s = None
    lru_p, lru_s, conv_p, conv_s, s5r_p, s5r_s, s5i_p, s5i_s = [], [], [], [], [], [], [], []
    for l in range(depth):
        xp = _ffn(xp, g_all, wg_all, wu_all, wd_all, l, 0)
        xs = _ffn(xs, g_all, wg_all, wu_all, wd_all, l, 0)
        gi = 3 * l + 1
        if l % 2 == 0:
            r = l // 2
            xp, cp, lp, srp, sip = _recurrent_layer(
                xp, bp, REC_TIME_CHUNK, g_all, gi, w_in_all, jnp.zeros((3 * bp, W_LRU), f32),
                jnp.zeros((bp, W_LRU), f32), jnp.zeros((bp, S5_N), f32), jnp.zeros((bp, S5_N), f32),
                rec_prm, glu_w_all, glu_b_all, w_out_all, r)
            conv0 = jnp.transpose(state_conv[r], (1, 0, 2)).reshape(3 * bs, W_LRU)
            xs, cs, ls, srs, sis = _recurrent_layer(
                xs, bs, ts, g_all, gi, w_in_all, conv0, state_lru_h[r], state_s5_re[r].reshape(bs, S5_N),
                state_s5_im[r].reshape(bs, S5_N), rec_prm, glu_w_all, glu_b_all, w_out_all, r)
            conv_p.append(jnp.transpose(cp.reshape(3, bp, W_LRU), (1, 0, 2)))
            conv_s.append(jnp.transpose(cs.reshape(3, bs, W_LRU), (1, 0, 2)))
            lru_p.append(lp); lru_s.append(ls)
            s5r_p.append(srp.reshape(bp, S5_GROUPS, S5_STATE)); s5r_s.append(srs.reshape(bs, S5_GROUPS, S5_STATE))
            s5i_p.append(sip.reshape(bp, S5_GROUPS, S5_STATE)); s5i_s.append(sis.reshape(bs, S5_GROUPS, S5_STATE))
        else:
            a = l // 2
            qb, qf, kt_p, vt_p, km = _qkv(xp, g_all, gi, wt_all, qg_all, kg_all, a, bp, kv_p)
            kv_p = (kt_p, vt_p)
            op = _attn_prompt(qb, qf, kt_p, vt_p, a, _group_block_means(km, bp, tp), bias_p)
            xp = _att_out(xp, op, w_o_all, a)
            xs_tb = _to_time_major(xs.reshape(bs, ts, D_MODEL))
            _, qf_s, kt_s, vt_s, _ = _qkv(xs_tb, g_all, gi, wt_all, qg_all, kg_all, a, 1, kv_s)
            kv_s = (kt_s, vt_s)
            q_s = jnp.transpose(qf_s.reshape(ts, bs, D_MODEL), (1, 0, 2))
            kn = jnp.transpose(kt_s[a, 0].reshape(D_MODEL, ts, bs), (2, 1, 0))
            vn = jnp.transpose(vt_s[a, 0].reshape(D_MODEL, ts, bs), (2, 1, 0))
            os_ = _attn_sample(a, page_table, q_s, kn, vn, bias_s, ckt, cvt)
            xs = _att_out(xs, os_.reshape(bs * ts, D_MODEL), w_o_all, a)
        xp = _ffn(xp, g_all, wg_all, wu_all, wd_all, l, 1)
        xs = _ffn(xs, g_all, wg_all, wu_all, wd_all, l, 1)

    n_att = depth // 2
    k_p = jnp.transpose(kv_p[0].reshape(n_att, bp, N_HEADS, HEAD_DIM, tp), (0, 1, 4, 2, 3))
    v_p = jnp.transpose(kv_p[1].reshape(n_att, bp, N_HEADS, HEAD_DIM, tp), (0, 1, 4, 2, 3))
    k_s = jnp.transpose(kv_s[0].reshape(n_att, N_HEADS, HEAD_DIM, ts, bs), (0, 4, 3, 1, 2))
    v_s = jnp.transpose(kv_s[1].reshape(n_att, N_HEADS, HEAD_DIM, ts, bs), (0, 4, 3, 1, 2))
    return (xp.reshape(bp, tp, D_MODEL), xs.reshape(bs, ts, D_MODEL), k_p, v_p, k_s, v_s,
            jnp.stack(lru_p), jnp.stack(lru_s), jnp.stack(conv_p), jnp.stack(conv_s),
            jnp.stack(s5r_p), jnp.stack(s5r_s), jnp.stack(s5i_p), jnp.stack(s5i_s))
```

```python
import functools
import math

import numpy as np
import jax
import jax.numpy as jnp
from jax import lax
from jax.experimental import pallas as pl
from jax.experimental.pallas import tpu as pltpu

f32 = jnp.float32
bf16 = jnp.bfloat16

D_MODEL = 1024
D_FF = 2816
W_LRU = 512
W_S5 = 512
CONV_W = 4
LRU_C = 8.0
S5_GROUPS = 32
S5_STATE = 64
S5_N = S5_GROUPS * S5_STATE
N_HEADS = 16
HEAD_DIM = 64
MOBA_BLOCK = 256
MOBA_TOPK = 3
REL_BUCKETS = 32
REL_MAX_DIST = 128
PAGE_SIZE = 128
EPS = 1e-6

TM = 512
FF_CHUNK = 256
REC_TIME_CHUNK = 64
ATT_HEAD_GROUP = 4
ATT_Q_BLOCKS = 2
ATT_ROW_CHUNK = 64
NEG = -1e30
VMEM_LIMIT = 56 * 1024 * 1024

_DN_T = (((1,), (1,)), ((), ()))


def _cparams(*sem):
    return pltpu.CompilerParams(dimension_semantics=sem, vmem_limit_bytes=VMEM_LIMIT)


def _layer_spec(shape, idx):
    return pl.BlockSpec((None,) + shape, lambda *_: (idx,) + (0,) * len(shape))


def _rms(x, g):
    ms = jnp.mean(x * x, axis=-1, keepdims=True)
    return x * lax.rsqrt(ms + EPS) * g


def _ffn_body(x_ref, g_ref, wg_ref, wu_ref, wd_ref, o_ref):
    x = x_ref[...]
    xn = _rms(x, g_ref[...]).astype(bf16)
    d = None
    for c in range(D_FF // FF_CHUNK):
        cs = slice(c * FF_CHUNK, (c + 1) * FF_CHUNK)
        gate = jnp.dot(xn, wg_ref[:, cs], preferred_element_type=f32)
        up = jnp.dot(xn, wu_ref[:, cs], preferred_element_type=f32)
        h = (gate * jax.nn.sigmoid(gate) * up).astype(bf16)
        dc = jnp.dot(h, wd_ref[cs, :], preferred_element_type=f32)
        d = dc if d is None else d + dc
    o_ref[...] = x + 0.5 * d


def _ffn(x, g, wg, wu, wd, l, s):
    n = x.shape[0]
    return pl.pallas_call(
        _ffn_body,
        grid=(n // TM,),
        in_specs=[
            pl.BlockSpec((TM, D_MODEL), lambda i: (i, 0)),
            _layer_spec((1, D_MODEL), 3 * l + 2 * s),
            _layer_spec((D_MODEL, D_FF), 2 * l + s),
            _layer_spec((D_MODEL, D_FF), 2 * l + s),
            _layer_spec((D_FF, D_MODEL), 2 * l + s),
        ],
        out_specs=pl.BlockSpec((TM, D_MODEL), lambda i: (i, 0)),
        out_shape=jax.ShapeDtypeStruct((n, D_MODEL), f32),
        compiler_params=_cparams("parallel"),
        name="ffn",
    )(x, g, wg, wu, wd)


def _att_out_body(x_ref, y_ref, w_ref, o_ref):
    o_ref[...] = x_ref[...] + jnp.dot(y_ref[...].astype(bf16), w_ref[...], preferred_element_type=f32)


def _att_out(x, y, w, a):
    n = x.shape[0]
    return pl.pallas_call(
        _att_out_body,
        grid=(n // TM,),
        in_specs=[
            pl.BlockSpec((TM, D_MODEL), lambda i: (i, 0)),
            pl.BlockSpec((TM, D_MODEL), lambda i: (i, 0)),
            _layer_spec((D_MODEL, D_MODEL), a),
        ],
        out_specs=pl.BlockSpec((TM, D_MODEL), lambda i: (i, 0)),
        out_shape=jax.ShapeDtypeStruct((n, D_MODEL), f32),
        compiler_params=_cparams("parallel"),
        name="att_out",
    )(x, y, w)


def _rec_out_body(x_ref, y_ref, gw_ref, gb_ref, w_ref, o_ref):
    y = y_ref[...]
    ya = y[:, :W_LRU].astype(bf16)
    g5 = y[:, W_LRU:]
    glu = jax.nn.sigmoid(jnp.dot(g5.astype(bf16), gw_ref[...], preferred_element_type=f32) + gb_ref[...])
    yb = (g5 * glu).astype(bf16)
    yy = jnp.concatenate([ya, yb], axis=-1)
    o_ref[...] = x_ref[...] + jnp.dot(yy, w_ref[...], preferred_element_type=f32)


def _rec_out(x, y, glu_w, glu_b, w_out, r):
    n = x.shape[0]
    return pl.pallas_call(
        _rec_out_body,
        grid=(n // TM,),
        in_specs=[
            pl.BlockSpec((TM, D_MODEL), lambda i: (i, 0)),
            pl.BlockSpec((TM, D_MODEL), lambda i: (i, 0)),
            _layer_spec((W_S5, W_S5), r),
            _layer_spec((1, W_S5), r),
            _layer_spec((D_MODEL, D_MODEL), r),
        ],
        out_specs=pl.BlockSpec((TM, D_MODEL), lambda i: (i, 0)),
        out_shape=jax.ShapeDtypeStruct((n, D_MODEL), f32),
        compiler_params=_cparams("parallel"),
        name="rec_out",
    )(x, y, glu_w, glu_b, w_out)


def _rec_scan_body(bsz, tc, x_ref, g_ref, win_ref, conv0_ref, lru0_ref, s5r0_ref, s5i0_ref, cw_ref, cb_ref, wr_ref,
                   br_ref, wi_ref, bi_ref, lam_ref, ar_ref, ai_ref, bbr_ref, bbi_ref, ccr_ref, cci_ref, d_ref,
                   y_ref, conv_out, lru_out, s5r_out, s5i_out,
                   z_ref, xbuf, a_scr, u_scr, hr_scr, hi_scr, lru_c, s5r_c, s5i_c):
    rows = bsz * tc
    step = pl.program_id(0)
    last = pl.num_programs(0) - 1
    z_ref[...] = jnp.dot(_rms(x_ref[...], g_ref[...]).astype(bf16), win_ref[...], preferred_element_type=f32)

    @pl.when(step == 0)
    def _():
        xbuf[0:3 * bsz, :] = conv0_ref[...]
        lru_c[...] = lru0_ref[...]
        s5r_c[...] = s5r0_ref[...]
        s5i_c[...] = s5i0_ref[...]

    xbuf[3 * bsz:, :] = z_ref[:, 0:W_LRU]
    xc = cb_ref[...] + cw_ref[0:1, :] * xbuf[0:rows, :]
    for j in range(1, CONV_W):
        xc = xc + cw_ref[j:j + 1, :] * xbuf[j * bsz:j * bsz + rows, :]
    tail = xbuf[rows:rows + 3 * bsz, :]

    xcb = xc.astype(bf16)
    r = jax.nn.sigmoid(jnp.dot(xcb, wr_ref[...], preferred_element_type=f32) + br_ref[...])
    ig = jax.nn.sigmoid(jnp.dot(xcb, wi_ref[...], preferred_element_type=f32) + bi_ref[...])
    lam = lam_ref[...]
    log_sig = jnp.minimum(lam, 0.0) - jnp.log1p(jnp.exp(-jnp.abs(lam)))
    log_a = (LRU_C * r) * log_sig
    a = jnp.exp(log_a)
    mult = jnp.sqrt(-jnp.tanh(log_a) * (a * a + 1.0))
    a_scr[...] = a
    u_scr[...] = mult * (ig * xc)

    def lru_step(t, h):
        rs = pl.ds(pl.multiple_of(t * bsz, 8), bsz)
        h = a_scr[rs, :] * h + u_scr[rs, :]
        u_scr[rs, :] = h
        return h

    h_last = lax.fori_loop(0, tc, lru_step, lru_c[...], unroll=4)
    lru_c[...] = h_last
    y_ref[:, 0:W_LRU] = u_scr[...] * jax.nn.gelu(z_ref[:, W_LRU:2 * W_LRU])

    half_in, half_st = W_S5 // 2, S5_N // 2
    for k in range(2):
        ubk = z_ref[:, 2 * W_LRU + k * half_in:2 * W_LRU + (k + 1) * half_in].astype(bf16)
        hr_scr[:, k * half_st:(k + 1) * half_st] = jnp.dot(ubk, bbr_ref[k], preferred_element_type=f32)
        hi_scr[:, k * half_st:(k + 1) * half_st] = jnp.dot(ubk, bbi_ref[k], preferred_element_type=f32)

    lane_chunk = 512
    for c in range(S5_N // lane_chunk):
        cs = slice(c * lane_chunk, (c + 1) * lane_chunk)
        ar = jnp.broadcast_to(ar_ref[:, cs], (bsz, lane_chunk))
        ai = jnp.broadcast_to(ai_ref[:, cs], (bsz, lane_chunk))

        def s5_step(t, carry, cs=cs, ar=ar, ai=ai):
            hr, hi = carry
            rs = pl.ds(pl.multiple_of(t * bsz, 8), bsz)
            nr = ar * hr - ai * hi + hr_scr[rs, cs]
            ni = ar * hi + ai * hr + hi_scr[rs, cs]
            hr_scr[rs, cs] = nr
            hi_scr[rs, cs] = ni
            return nr, ni

        hr, hi = lax.fori_loop(0, tc, s5_step, (s5r_c[:, cs], s5i_c[:, cs]), unroll=4)
        s5r_c[:, cs] = hr
        s5i_c[:, cs] = hi

    for k in range(2):
        hrk = hr_scr[:, k * half_st:(k + 1) * half_st].astype(bf16)
        hik = hi_scr[:, k * half_st:(k + 1) * half_st].astype(bf16)
        ys = (jnp.dot(hrk, ccr_ref[k], preferred_element_type=f32)
              - jnp.dot(hik, cci_ref[k], preferred_element_type=f32))
        ub = z_ref[:, 2 * W_LRU + k * half_in:2 * W_LRU + (k + 1) * half_in]
        ys = ys + d_ref[:, k * half_in:(k + 1) * half_in] * ub
        y_ref[:, W_LRU + k * half_in:W_LRU + (k + 1) * half_in] = jax.nn.gelu(ys)

    xbuf[0:3 * bsz, :] = tail

    @pl.when(step == last)
    def _():
        conv_out[...] = tail
        lru_out[...] = lru_c[...]
        s5r_out[...] = s5r_c[...]
        s5i_out[...] = s5i_c[...]


def _rec_scan(x, bsz, tc, g, gi, w_in, conv0, lru0, s5r0, s5i0, prm, r):
    n = x.shape[0]
    rows = bsz * tc
    full = lambda shape: pl.BlockSpec(shape, lambda i: (0,) * len(shape))
    in_specs = [
        pl.BlockSpec((rows, D_MODEL), lambda i: (i, 0)),
        _layer_spec((1, D_MODEL), gi),
        _layer_spec((D_MODEL, 3 * W_LRU), r),
        full((3 * bsz, W_LRU)), full((bsz, W_LRU)), full((bsz, S5_N)), full((bsz, S5_N)),
    ] + [_layer_spec(p.shape[1:], r) for p in prm]
    out_shape = (
        jax.ShapeDtypeStruct((n, D_MODEL), f32),
        jax.ShapeDtypeStruct((3 * bsz, W_LRU), f32),
        jax.ShapeDtypeStruct((bsz, W_LRU), f32),
        jax.ShapeDtypeStruct((bsz, S5_N), f32),
        jax.ShapeDtypeStruct((bsz, S5_N), f32),
    )
    out_specs = (
        pl.BlockSpec((rows, D_MODEL), lambda i: (i, 0)),
        full((3 * bsz, W_LRU)), full((bsz, W_LRU)), full((bsz, S5_N)), full((bsz, S5_N)),
    )
    scratch = [
        pltpu.VMEM((rows, 3 * W_LRU), f32),
        pltpu.VMEM((rows + 3 * bsz, W_LRU), f32),
        pltpu.VMEM((rows, W_LRU), f32), pltpu.VMEM((rows, W_LRU), f32),
        pltpu.VMEM((rows, S5_N), f32), pltpu.VMEM((rows, S5_N), f32),
        pltpu.VMEM((bsz, W_LRU), f32), pltpu.VMEM((bsz, S5_N), f32), pltpu.VMEM((bsz, S5_N), f32),
    ]
    return pl.pallas_call(
        functools.partial(_rec_scan_body, bsz, tc),
        grid=(n // rows,),
        in_specs=in_specs,
        out_specs=out_specs,
        out_shape=out_shape,
        scratch_shapes=scratch,
        compiler_params=_cparams("arbitrary"),
        name="rec_scan",
    )(x, g, w_in, conv0, lru0, s5r0, s5i0, *prm)


def _head_norm_t(t, gain):
    n = t.shape[1]
    t3 = t.reshape(N_HEADS, HEAD_DIM, n)
    ms = jnp.mean(t3 * t3, axis=1, keepdims=True)
    return (t3 * lax.rsqrt(ms + EPS) * gain[None]).reshape(N_HEADS * HEAD_DIM, n)


def _qkv_body(n_alias, x_ref, g_ref, wt_ref, qg_ref, kg_ref, *refs):
    qb_ref, qf_ref, kt_ref, vt_ref, km_ref = refs[n_alias:]
    xn = _rms(x_ref[...], g_ref[...])
    xnt = xn.T.astype(bf16)
    qkvt = jnp.dot(wt_ref[...], xnt, preferred_element_type=f32)
    qt = _head_norm_t(qkvt[0:D_MODEL], qg_ref[...])
    kt = _head_norm_t(qkvt[D_MODEL:2 * D_MODEL], kg_ref[...])
    q = qt.T
    qf_ref[...] = q
    qb_ref[...] = (q * (HEAD_DIM ** -0.5)).astype(bf16)
    if n_alias:
        kt_ref[0] = kt
        vt_ref[0] = qkvt[2 * D_MODEL:]
    else:
        for j in range(kt_ref.shape[0]):
            kt_ref[j, 0] = kt
            vt_ref[j, 0] = qkvt[2 * D_MODEL:]
    for n in range(TM // MOBA_BLOCK):
        km_ref[0, :, n:n + 1] = jnp.mean(kt[:, n * MOBA_BLOCK:(n + 1) * MOBA_BLOCK], axis=1, keepdims=True)


def _qkv(x, g, gi, wt, qg, kg, a, nb, kv_prev=None):
    n = x.shape[0]
    tlen = n // nb
    tpb = tlen // TM
    n_att = wt.shape[0]
    prev = () if kv_prev is None else tuple(kv_prev)
    if prev:
        kv_spec = pl.BlockSpec((None, 1, D_MODEL, TM), lambda i: (a, i // tpb, 0, i % tpb))
    else:
        kv_spec = pl.BlockSpec((n_att, 1, D_MODEL, TM), lambda i: (0, i // tpb, 0, i % tpb))
    return pl.pallas_call(
        functools.partial(_qkv_body, len(prev)),
        grid=(n // TM,),
        in_specs=[
            pl.BlockSpec((TM, D_MODEL), lambda i: (i, 0)),
            _layer_spec((1, D_MODEL), gi),
            _layer_spec((3 * D_MODEL, D_MODEL), a),
            _layer_spec((HEAD_DIM, TM), a),
            _layer_spec((HEAD_DIM, TM), a),
        ] + [pl.BlockSpec(memory_space=pl.ANY)] * len(prev),
        out_specs=(
            pl.BlockSpec((TM, D_MODEL), lambda i: (i, 0)),
            pl.BlockSpec((TM, D_MODEL), lambda i: (i, 0)),
            kv_spec,
            kv_spec,
            pl.BlockSpec((1, D_MODEL, TM // MOBA_BLOCK), lambda i: (i, 0, 0)),
        ),
        out_shape=(
            jax.ShapeDtypeStruct((n, D_MODEL), bf16),
            jax.ShapeDtypeStruct((n, D_MODEL), f32),
            jax.ShapeDtypeStruct((n_att, nb, D_MODEL, tlen), f32),
            jax.ShapeDtypeStruct((n_att, nb, D_MODEL, tlen), f32),
            jax.ShapeDtypeStruct((n // TM, D_MODEL, TM // MOBA_BLOCK), f32),
        ),
        input_output_aliases={5 + j: 2 + j for j in range(len(prev))},
        compiler_params=_cparams("parallel"),
        name="qkv",
    )(x, g, wt, qg, kg, *prev)


def _split_bf16(x):
    hi = x.astype(bf16)
    lo = (x - hi.astype(f32)).astype(bf16)
    return hi, lo


def _attn_prompt_body(nblk, hg, nqb, qb_ref, qf_ref, kt_ref, vt_ref, km_ref, bias_ref, o_ref,
                      ka_scr, va_scr, qa_scr, s_scr, p_scr):
    step = pl.program_id(2)
    pw = 2 * HEAD_DIM
    tlen = kt_ref.shape[2]
    qrows = nqb * MOBA_BLOCK
    lane = lax.broadcasted_iota(jnp.int32, (1, pw), 1)
    nio = lax.broadcasted_iota(jnp.int32, (nblk, MOBA_BLOCK), 0)

    @pl.when(step == 0)
    def _():
        blk_rows = lax.broadcasted_iota(jnp.int32, (pw, tlen), 0)
        blk_cols = lax.broadcasted_iota(jnp.int32, (pw, tlen), 1) // MOBA_BLOCK
        onehot = jnp.where((blk_rows == blk_cols) | (blk_rows == blk_cols + nblk), 1.0, 0.0).astype(bf16)
        for pp in range(hg // 2):
            ka_scr[pp, 0:pw, :] = kt_ref[0, pp * pw:(pp + 1) * pw, :].astype(bf16)
            ka_scr[pp, pw:2 * pw, :] = onehot
            va_scr[pp, 0:pw, :] = vt_ref[0, pp * pw:(pp + 1) * pw, :].astype(bf16)
            va_scr[pp, pw:2 * pw, :] = jnp.ones((pw, tlen), bf16)

    for h in range(hg):
        ps = slice((h // 2) * pw, (h // 2 + 1) * pw)
        hmask = (lane // HEAD_DIM) == h % 2
        k_hi, k_lo = _split_bf16(km_ref[0, 0, :, ps])
        b_far = bias_ref[0, h, 2, 0:1, 0:1]
        b_hi = b_far.astype(bf16).astype(f32)
        for qi in range(nqb):
            i = step * nqb + qi
            rows = slice(qi * MOBA_BLOCK, (qi + 1) * MOBA_BLOCK)
            qa_scr[h, rows, 0:pw] = jnp.where(hmask, qb_ref[rows, ps], jnp.zeros((), bf16))
            q_hi, q_lo = _split_bf16(jnp.where(hmask, qf_ref[rows, ps], 0.0))
            gate = (lax.dot_general(k_hi, q_hi, _DN_T, preferred_element_type=f32)
                    + lax.dot_general(k_lo, q_hi, _DN_T, preferred_element_type=f32)
                    + lax.dot_general(k_hi, q_lo, _DN_T, preferred_element_type=f32))[0:nblk]
            rank = jnp.zeros((nblk, MOBA_BLOCK), f32)
            for m in range(nblk):
                gm = gate[m:m + 1, :]
                beats = jnp.where(gm > gate, 1.0, jnp.where(gm == gate, jnp.where(m < nio, 1.0, 0.0), 0.0))
                rank = rank + jnp.where(m < i, beats, 0.0)
            keep = jnp.where(nio < i, jnp.where(rank < MOBA_TOPK, 1.0, 0.0), jnp.where(nio == i, 1.0, 0.0)) > 0.5
            far = nio < i - 1
            term_hi = jnp.where(keep, jnp.where(far, b_hi, 0.0), NEG)
            term_lo = jnp.where(keep & far, b_far - b_hi, 0.0)
            terms = jnp.concatenate([term_hi, term_lo, jnp.zeros((pw - 2 * nblk, MOBA_BLOCK), f32)], axis=0).T
            qa_scr[h, rows, pw:2 * pw] = terms.astype(bf16)

    nrow = 2 * qrows

    def process(width):
        for pp in range(hg // 2):
            qa = qa_scr[2 * pp:2 * pp + 2].reshape(nrow, 2 * pw)
            s_scr[pp, :, 0:width] = jnp.dot(qa, ka_scr[pp, :, 0:width], preferred_element_type=f32)
        for h in range(hg):
            for qi in range(nqb):
                i = step * nqb + qi
                rows = slice((h % 2) * qrows + qi * MOBA_BLOCK, (h % 2) * qrows + (qi + 1) * MOBA_BLOCK)
                near = pl.multiple_of(i * MOBA_BLOCK, MOBA_BLOCK)
                prev = pl.multiple_of(jnp.maximum(i - 1, 0) * MOBA_BLOCK, MOBA_BLOCK)
                prev_tile = jnp.where(i >= 1, 1, 3)
                s_scr[h // 2, rows, pl.ds(near, MOBA_BLOCK)] += bias_ref[0, h, 0]
                s_scr[h // 2, rows, pl.ds(prev, MOBA_BLOCK)] += bias_ref[0, h, prev_tile]
        for pp in range(hg // 2):
            for r in range(nrow // ATT_ROW_CHUNK):
                rs = slice(r * ATT_ROW_CHUNK, (r + 1) * ATT_ROW_CHUNK)
                mx = s_scr[pp, rs, 0:MOBA_BLOCK]
                for t in range(1, width // MOBA_BLOCK):
                    mx = jnp.maximum(mx, s_scr[pp, rs, t * MOBA_BLOCK:(t + 1) * MOBA_BLOCK])
                mx = jnp.max(mx, axis=-1, keepdims=True)
                for t in range(width // MOBA_BLOCK):
                    cs = slice(t * MOBA_BLOCK, (t + 1) * MOBA_BLOCK)
                    p_scr[pp, rs, cs] = jnp.exp(s_scr[pp, rs, cs] - mx).astype(bf16)
        for pp in range(hg // 2):
            o = lax.dot_general(p_scr[pp, :, 0:width], va_scr[pp, :, 0:width], _DN_T, preferred_element_type=f32)
            res = [o[sub * qrows:(sub + 1) * qrows, 0:pw] / o[sub * qrows:(sub + 1) * qrows, pw:2 * pw]
                   for sub in range(2)]
            o_ref[:, pp * pw:(pp + 1) * pw] = jnp.where(lane < HEAD_DIM, res[0], res[1]).astype(o_ref.dtype)

    for k in range(nblk // nqb):
        @pl.when(step == k)
        def _(k=k):
            process((k + 1) * qrows)


def _attn_prompt(qb, qf, kt, vt, a, km, bias):
    _, nb, _, tlen = kt.shape
    nq = tlen // MOBA_BLOCK
    hg, nqb = ATT_HEAD_GROUP, ATT_Q_BLOCKS
    gw = hg * HEAD_DIM
    qrows = nqb * MOBA_BLOCK
    steps = nq // nqb
    kv_spec = pl.BlockSpec((None, 1, gw, tlen), lambda p, b, i: (a, b, p, 0))
    return pl.pallas_call(
        functools.partial(_attn_prompt_body, nq, hg, nqb),
        grid=(N_HEADS // hg, nb, steps),
        in_specs=[
            pl.BlockSpec((qrows, gw), lambda p, b, i: (b * steps + i, p)),
            pl.BlockSpec((qrows, gw), lambda p, b, i: (b * steps + i, p)),
            kv_spec,
            kv_spec,
            pl.BlockSpec((1, 1, km.shape[2], gw), lambda p, b, i: (b, p, 0, 0)),
            pl.BlockSpec((1, hg, 4, MOBA_BLOCK, MOBA_BLOCK), lambda p, b, i: (p, 0, 0, 0, 0)),
        ],
        out_specs=pl.BlockSpec((qrows, gw), lambda p, b, i: (b * steps + i, p)),
        out_shape=jax.ShapeDtypeStruct((nb * tlen, D_MODEL), bf16),
        scratch_shapes=[
            pltpu.VMEM((hg // 2, 4 * HEAD_DIM, tlen), bf16),
            pltpu.VMEM((hg // 2, 4 * HEAD_DIM, tlen), bf16),
            pltpu.VMEM((hg, qrows, 4 * HEAD_DIM), bf16),
            pltpu.VMEM((hg // 2, 2 * qrows, tlen), f32),
            pltpu.VMEM((hg // 2, 2 * qrows, tlen), bf16),
        ],
        compiler_params=_cparams("parallel", "parallel", "arbitrary"),
        name="attn_prompt",
    )(qb, qf, kt, vt, km, bias)


def _attn_sample_body(layer, n_pages, pt_ref, q_ref, kn_ref, vn_ref, bias_ref, ck_hbm, cv_hbm, o_ref,
                      kbuf, vbuf, sems, s_scr, p_scr):
    b = pl.program_id(0)
    slot = b % 2
    nq = q_ref.shape[1]
    npg = n_pages + 1
    rows = N_HEADS * 8

    def k_copy(bb, sl, p):
        return pltpu.make_async_copy(ck_hbm.at[layer, pt_ref[bb, p]], kbuf.at[sl, p], sems.at[0, sl, p])

    def v_copy(bb, sl, p):
        return pltpu.make_async_copy(cv_hbm.at[layer, pt_ref[bb, p]], vbuf.at[sl, p], sems.at[1, sl, p])

    def start_all(bb, sl):
        for p in range(n_pages):
            k_copy(bb, sl, p).start()
        for p in range(n_pages):
            v_copy(bb, sl, p).start()

    @pl.when(b == 0)
    def _():
        start_all(0, 0)

    @pl.when(b + 1 < pl.num_programs(0))
    def _():
        start_all(b + 1, 1 - slot)

    q8 = jnp.concatenate([q_ref[0] * (HEAD_DIM ** -0.5), jnp.zeros((8 - nq, D_MODEL), f32)], axis=0)
    rh = lax.broadcasted_iota(jnp.int32, (rows, D_MODEL), 0) // 8
    ch = lax.broadcasted_iota(jnp.int32, (rows, D_MODEL), 1) // HEAD_DIM
    qblk = jnp.where(rh == ch, jnp.tile(q8, (N_HEADS, 1)), 0.0).astype(bf16)

    def page_scores(kpage):
        kp = kpage.reshape(N_HEADS * HEAD_DIM, PAGE_SIZE).astype(bf16)
        return jnp.dot(qblk, kp, preferred_element_type=f32).reshape(N_HEADS, 8, PAGE_SIZE)

    for p in range(n_pages):
        k_copy(b, slot, p).wait()
    for p in range(n_pages):
        s_scr[p] = page_scores(kbuf[slot, p])

    def new_page(ref):
        return jnp.concatenate([ref[0], jnp.zeros((PAGE_SIZE - nq, D_MODEL), f32)], axis=0).astype(bf16)

    s_scr[n_pages] = lax.dot_general(qblk, new_page(kn_ref), _DN_T,
                                     preferred_element_type=f32).reshape(N_HEADS, 8, PAGE_SIZE)

    ppb = MOBA_BLOCK // PAGE_SIZE
    nblk = n_pages // ppb
    gates = []
    for n in range(nblk):
        blk = s_scr[n * ppb]
        for e in range(1, ppb):
            blk = blk + s_scr[n * ppb + e]
        gates.append(jnp.sum(blk, axis=-1, keepdims=True))
    sel = []
    for n in range(nblk):
        rank = jnp.zeros_like(gates[n])
        for m in range(nblk):
            if m == n:
                continue
            if m < n:
                rank = rank + jnp.where(gates[m] >= gates[n], 1.0, 0.0)
            else:
                rank = rank + jnp.where(gates[m] > gates[n], 1.0, 0.0)
        sel.append(rank < MOBA_TOPK)

    mx_e = jnp.full((N_HEADS, 8, PAGE_SIZE), -1e38, f32)
    for p in range(npg):
        s = s_scr[p] + bias_ref[p]
        if p < n_pages:
            s = jnp.where(sel[p // ppb], s, NEG)
        s_scr[p] = s
        mx_e = jnp.maximum(mx_e, s)
    mx = jnp.max(mx_e, axis=-1, keepdims=True)
    den_e = jnp.zeros((N_HEADS, 8, PAGE_SIZE), f32)
    for p in range(npg):
        e = jnp.exp(s_scr[p] - mx)
        p_scr[p] = e
        den_e = den_e + e
    inv = 1.0 / jnp.sum(den_e, axis=-1, keepdims=True)

    for p in range(n_pages):
        v_copy(b, slot, p).wait()

    def page_pv(p, vpage):
        pr = (p_scr[p] * inv).reshape(rows, PAGE_SIZE).astype(bf16)
        vp = vpage.reshape(N_HEADS * HEAD_DIM, PAGE_SIZE).astype(bf16)
        return lax.dot_general(pr, vp, _DN_T, preferred_element_type=f32)

    pr_new = (p_scr[n_pages] * inv).reshape(rows, PAGE_SIZE).astype(bf16)
    acc = jnp.dot(pr_new, new_page(vn_ref), preferred_element_type=f32)
    for p in range(n_pages):
        acc = acc + page_pv(p, vbuf[slot, p])
    lane_head = lax.broadcasted_iota(jnp.int32, (8, D_MODEL), 1) // HEAD_DIM
    out = jnp.zeros((8, D_MODEL), f32)
    for h in range(N_HEADS):
        out = out + jnp.where(lane_head == h, acc[h * 8:(h + 1) * 8, :], 0.0)
    o_ref[0] = out[0:nq, :].astype(o_ref.dtype)


def _attn_sample(layer, page_table, q, kn, vn, bias, ck, cv):
    nb, nq, _ = q.shape
    n_pages = page_table.shape[1]
    npg = n_pages + 1
    grid_spec = pltpu.PrefetchScalarGridSpec(
        num_scalar_prefetch=1,
        grid=(nb,),
        in_specs=[
            pl.BlockSpec((1, nq, D_MODEL), lambda b, pt: (b, 0, 0)),
            pl.BlockSpec((1, nq, D_MODEL), lambda b, pt: (b, 0, 0)),
            pl.BlockSpec((1, nq, D_MODEL), lambda b, pt: (b, 0, 0)),
            pl.BlockSpec((npg, N_HEADS, 8, PAGE_SIZE), lambda b, pt: (0, 0, 0, 0)),
            pl.BlockSpec(memory_space=pl.ANY),
            pl.BlockSpec(memory_space=pl.ANY),
        ],
        out_specs=pl.BlockSpec((1, nq, D_MODEL), lambda b, pt: (b, 0, 0)),
        scratch_shapes=[
            pltpu.VMEM((2, n_pages, N_HEADS, HEAD_DIM, PAGE_SIZE), f32),
            pltpu.VMEM((2, n_pages, N_HEADS, HEAD_DIM, PAGE_SIZE), f32),
            pltpu.SemaphoreType.DMA((2, 2, n_pages)),
            pltpu.VMEM((npg, N_HEADS, 8, PAGE_SIZE), f32),
            pltpu.VMEM((npg, N_HEADS, 8, PAGE_SIZE), f32),
        ],
    )
    return pl.pallas_call(
        functools.partial(_attn_sample_body, layer, n_pages),
        grid_spec=grid_spec,
        out_shape=jax.ShapeDtypeStruct((nb, nq, D_MODEL), f32),
        compiler_params=_cparams("arbitrary"),
        name="attn_sample",
    )(page_table, q, kn, vn, bias, ck, cv)


def _t5_bucket_np(dist):
    dist = np.asarray(dist)
    max_exact = REL_BUCKETS // 2
    log_ratio = (np.log(np.maximum(dist, 1).astype(np.float32) / np.float32(max_exact))
                 / np.float32(math.log(REL_MAX_DIST / max_exact))).astype(np.float32)
    large = np.minimum(max_exact + (log_ratio * np.float32(REL_BUCKETS - max_exact)).astype(np.int32), REL_BUCKETS - 1)
    return np.where(dist < max_exact, dist, large)


def _prompt_bias(rel_bias):
    blk, span = MOBA_BLOCK, 2 * MOBA_BLOCK
    by_dist = rel_bias[_t5_bucket_np(np.arange(3 * blk))].T
    j = np.arange(span)
    j = np.where(j < blk, j, j - span)
    tiles = []
    for d in range(3):
        dist = d * blk - j
        w = jnp.where((dist >= 0)[None, :], by_dist[:, np.clip(dist, 0, 3 * blk - 1)], NEG)
        t = jnp.tile(w, (1, blk))[:, :blk * (span - 1)].reshape(N_HEADS, blk, span - 1)
        tiles.append(t[:, :, :blk])
    tiles.append(jnp.zeros_like(tiles[0]))
    t = jnp.stack(tiles, axis=1)
    return t.reshape(N_HEADS // ATT_HEAD_GROUP, ATT_HEAD_GROUP, 4, blk, blk)


def _group_block_means(km, nb, tlen):
    gw = ATT_HEAD_GROUP * HEAD_DIM
    nblk = tlen // MOBA_BLOCK
    km = jnp.transpose(km.reshape(nb, tlen // TM, N_HEADS // ATT_HEAD_GROUP, gw, TM // MOBA_BLOCK), (0, 2, 1, 4, 3))
    km = km.reshape(nb, N_HEADS // ATT_HEAD_GROUP, nblk, gw)
    return jnp.pad(km, ((0, 0), (0, 0), (0, 16 - nblk), (0, 0)))


def _sample_bias(rel_bias, n_pages, nq):
    past = n_pages * PAGE_SIZE
    nkeys = (n_pages + 1) * PAGE_SIZE
    span = past + 8
    by_dist = rel_bias[_t5_bucket_np(np.arange(span))].T
    rev = jnp.pad(by_dist[:, ::-1], ((0, 0), (0, nkeys)))
    rows = jnp.stack([rev[:, 7 - t:7 - t + nkeys] for t in range(8)], axis=1)
    kpos = np.arange(nkeys)
    t8 = np.arange(8)[:, None]
    valid = (past + t8 - kpos[None, :] >= 0) & (t8 < nq) & (kpos[None, :] < past + nq)
    fill = np.where(t8 < nq, NEG, 0.0).astype(np.float32)
    b = jnp.where(valid[None], rows, fill[None])
    return jnp.transpose(b.reshape(N_HEADS, 8, n_pages + 1, PAGE_SIZE), (2, 0, 1, 3))


def _block_diag(w):
    g, a, b = w.shape
    eye = jnp.eye(g, dtype=w.dtype)
    return jnp.einsum('gab,gh->gahb', w, eye).reshape(g * a, g * b)


def _rec_params(conv_w, conv_b, w_r, b_r, w_i, b_i, lam, lam_re, lam_im, log_step, b_re, b_im, c_re, c_im, d):
    dt = jnp.exp(log_step)[:, None]
    mag = jnp.exp(lam_re * dt)
    ar, ai = mag * jnp.cos(lam_im * dt), mag * jnp.sin(lam_im * dt)
    den = lam_re * lam_re + lam_im * lam_im
    qr = ((ar - 1.0) * lam_re + ai * lam_im) / den
    qi = (ai * lam_re - (ar - 1.0) * lam_im) / den
    bbr = qr[..., None] * b_re - qi[..., None] * b_im
    bbi = qr[..., None] * b_im + qi[..., None] * b_re

    def halves(m):
        return jnp.stack([m[:m.shape[0] // 2, :m.shape[1] // 2], m[m.shape[0] // 2:, m.shape[1] // 2:]]).astype(bf16)

    bbr_m = halves(_block_diag(jnp.transpose(bbr, (0, 2, 1))))
    bbi_m = halves(_block_diag(jnp.transpose(bbi, (0, 2, 1))))
    ccr_m = halves(_block_diag(jnp.transpose(c_re, (0, 2, 1))))
    cci_m = halves(_block_diag(jnp.transpose(c_im, (0, 2, 1))))
    return (conv_w, conv_b[None], _block_diag(w_r).astype(bf16), b_r[None], _block_diag(w_i).astype(bf16), b_i[None],
            lam[None], ar.reshape(1, S5_N), ai.reshape(1, S5_N), bbr_m, bbi_m, ccr_m, cci_m, d[None])


def _to_time_major(x):
    b, t, c = x.shape
    return jnp.transpose(x, (1, 0, 2)).reshape(t * b, c)


def _from_time_major(y, b):
    n, c = y.shape
    return jnp.transpose(y.reshape(n // b, b, c), (1, 0, 2)).reshape(n, c)


def _recurrent_layer(x2d, bsz, tc, g, gi, w_in, conv0, lru0, s5r0, s5i0, prm, glu_w, glu_b, w_out, r):
    n = x2d.shape[0]
    x_tb = _to_time_major(x2d.reshape(bsz, n // bsz, D_MODEL))
    y_tb, conv_n, lru_n, s5r_n, s5i_n = _rec_scan(x_tb, bsz, tc, g, gi, w_in, conv0, lru0, s5r0, s5i0, prm, r)
    y = _from_time_major(y_tb, bsz)
    return _rec_out(x2d, y, glu_w, glu_b, w_out, r), conv_n, lru_n, s5r_n, s5i_n


def kernel(x_prompt, x_sample, cache_k, cache_v, state_lru_h, state_conv, state_s5_re, state_s5_im, page_table, norm_g, ffn_w_gate, ffn_w_up, ffn_w_down, rec_w_in, rec_conv_w, rec_conv_b, lru_w_r, lru_b_r, lru_w_i, lru_b_i, lru_lambda, s5_lambda_re, s5_lambda_im, s5_log_step, s5_b_re, s5_b_im, s5_c_re, s5_c_im, s5_d, s5_glu_w, s5_glu_b, rec_w_out, att_w_qkv, att_q_norm, att_k_norm, att_w_o, rel_bias):
    bp, tp, _ = x_prompt.shape
    bs, ts, _ = x_sample.shape
    depth = norm_g.shape[0]
    n_pages = page_table.shape[1]
    xp = x_prompt.reshape(bp * tp, D_MODEL)
    xs = x_sample.reshape(bs * ts, D_MODEL)

    g_all = norm_g.reshape(depth * 3, 1, D_MODEL)
    wg_all = ffn_w_gate.astype(bf16).reshape(depth * 2, D_MODEL, D_FF)
    wu_all = ffn_w_up.astype(bf16).reshape(depth * 2, D_MODEL, D_FF)
    wd_all = ffn_w_down.astype(bf16).reshape(depth * 2, D_FF, D_MODEL)
    w_in_all = rec_w_in.astype(bf16)
    glu_w_all = s5_glu_w.astype(bf16)
    glu_b_all = s5_glu_b[:, None, :]
    w_out_all = rec_w_out.astype(bf16)
    rec_prm = jax.vmap(_rec_params)(rec_conv_w, rec_conv_b, lru_w_r, lru_b_r, lru_w_i, lru_b_i, lru_lambda,
                                    s5_lambda_re, s5_lambda_im, s5_log_step, s5_b_re, s5_b_im, s5_c_re, s5_c_im, s5_d)
    wt_all = jnp.transpose(att_w_qkv, (0, 2, 1)).astype(bf16)
    qg_all = jnp.broadcast_to(att_q_norm[:, :, None], att_q_norm.shape + (TM,))
    kg_all = jnp.broadcast_to(att_k_norm[:, :, None], att_k_norm.shape + (TM,))
    w_o_all = att_w_o.astype(bf16)
    ckt = jnp.transpose(cache_k, (0, 1, 3, 4, 2))
    cvt = jnp.transpose(cache_v, (0, 1, 3, 4, 2))
    bias_p = _prompt_bias(rel_bias)
    bias_s = _sample_bias(rel_bias, n_pages, ts)

    kv_p = kv_s = None
    lru_p, lru_s, conv_p, conv_s, s5r_p, s5r_s, s5i_p, s5i_s = [], [], [], [], [], [], [], []
    for l in range(depth):
        xp = _ffn(xp, g_all, wg_all, wu_all, wd_all, l, 0)
        xs = _ffn(xs, g_all, wg_all, wu_all, wd_all, l, 0)
        gi = 3 * l + 1
        if l % 2 == 0:
            r = l // 2
            xp, cp, lp, srp, sip = _recurrent_layer(
                xp, bp, REC_TIME_CHUNK, g_all, gi, w_in_all, jnp.zeros((3 * bp, W_LRU), f32),
                jnp.zeros((bp, W_LRU), f32), jnp.zeros((bp, S5_N), f32), jnp.zeros((bp, S5_N), f32),
                rec_prm, glu_w_all, glu_b_all, w_out_all, r)
            conv0 = jnp.transpose(state_conv[r], (1, 0, 2)).reshape(3 * bs, W_LRU)
            xs, cs, ls, srs, sis = _recurrent_layer(
                xs, bs, ts, g_all, gi, w_in_all, conv0, state_lru_h[r], state_s5_re[r].reshape(bs, S5_N),
                state_s5_im[r].reshape(bs, S5_N), rec_prm, glu_w_all, glu_b_all, w_out_all, r)
            conv_p.append(jnp.transpose(cp.reshape(3, bp, W_LRU), (1, 0, 2)))
            conv_s.append(jnp.transpose(cs.reshape(3, bs, W_LRU), (1, 0, 2)))
            lru_p.append(lp); lru_s.append(ls)
            s5r_p.append(srp.reshape(bp, S5_GROUPS, S5_STATE)); s5r_s.append(srs.reshape(bs, S5_GROUPS, S5_STATE))
            s5i_p.append(sip.reshape(bp, S5_GROUPS, S5_STATE)); s5i_s.append(sis.reshape(bs, S5_GROUPS, S5_STATE))
        else:
            a = l // 2
            qb, qf, kt_p, vt_p, km = _qkv(xp, g_all, gi, wt_all, qg_all, kg_all, a, bp, kv_p)
            kv_p = (kt_p, vt_p)
            op = _attn_prompt(qb, qf, kt_p, vt_p, a, _group_block_means(km, bp, tp), bias_p)
            xp = _att_out(xp, op, w_o_all, a)
            xs_tb = _to_time_major(xs.reshape(bs, ts, D_MODEL))
            _, qf_s, kt_s, vt_s, _ = _qkv(xs_tb, g_all, gi, wt_all, qg_all, kg_all, a, 1, kv_s)
            kv_s = (kt_s, vt_s)
            q_s = jnp.transpose(qf_s.reshape(ts, bs, D_MODEL), (1, 0, 2))
            kn = jnp.transpose(kt_s[a, 0].reshape(D_MODEL, ts, bs), (2, 1, 0))
            vn = jnp.transpose(vt_s[a, 0].reshape(D_MODEL, ts, bs), (2, 1, 0))
            os_ = _attn_sample(a, page_table, q_s, kn, vn, bias_s, ckt, cvt)
            xs = _att_out(xs, os_.reshape(bs * ts, D_MODEL), w_o_all, a)
        xp = _ffn(xp, g_all, wg_all, wu_all, wd_all, l, 1)
        xs = _ffn(xs, g_all, wg_all, wu_all, wd_all, l, 1)

    n_att = depth // 2
    k_p = jnp.transpose(kv_p[0].reshape(n_att, bp, N_HEADS, HEAD_DIM, tp), (0, 1, 4, 2, 3))
    v_p = jnp.transpose(kv_p[1].reshape(n_att, bp, N_HEADS, HEAD_DIM, tp), (0, 1, 4, 2, 3))
    k_s = jnp.transpose(kv_s[0].reshape(n_att, N_HEADS, HEAD_DIM, ts, bs), (0, 4, 3, 1, 2))
    v_s = jnp.transpose(kv_s[1].reshape(n_att, N_HEADS, HEAD_DIM, ts, bs), (0, 4, 3, 1, 2))
    return (xp.reshape(bp, tp, D_MODEL), xs.reshape(bs, ts, D_MODEL), k_p, v_p, k_s, v_s,
            jnp.stack(lru_p), jnp.stack(lru_s), jnp.stack(conv_p), jnp.stack(conv_s),
            jnp.stack(s5r_p), jnp.stack(s5r_s), jnp.stack(s5i_p), jnp.stack(s5i_s))
```

```python
import functools
import math

import numpy as np
import jax
import jax.numpy as jnp
from jax import lax
from jax.experimental import pallas as pl
from jax.experimental.pallas import tpu as pltpu

f32 = jnp.float32
bf16 = jnp.bfloat16

D_MODEL = 1024
D_FF = 2816
W_LRU = 512
W_S5 = 512
CONV_W = 4
LRU_C = 8.0
S5_GROUPS = 32
S5_STATE = 64
S5_N = S5_GROUPS * S5_STATE
N_HEADS = 16
HEAD_DIM = 64
MOBA_BLOCK = 256
MOBA_TOPK = 3
REL_BUCKETS = 32
REL_MAX_DIST = 128
PAGE_SIZE = 128
EPS = 1e-6

TM = 512
FF_CHUNK = 256
REC_TIME_CHUNK = 64
ATT_HEAD_GROUP = 4
ATT_Q_BLOCKS = 2
ATT_ROW_CHUNK = 64
NEG = -1e30
VMEM_LIMIT = 56 * 1024 * 1024

_DN_T = (((1,), (1,)), ((), ()))


def _cparams(*sem):
    return pltpu.CompilerParams(dimension_semantics=sem, vmem_limit_bytes=VMEM_LIMIT)


def _layer_spec(shape, idx):
    return pl.BlockSpec((None,) + shape, lambda *_: (idx,) + (0,) * len(shape))


def _rms(x, g):
    ms = jnp.mean(x * x, axis=-1, keepdims=True)
    return x * lax.rsqrt(ms + EPS) * g


def _ffn_body(x_ref, g_ref, wg_ref, wu_ref, wd_ref, o_ref):
    x = x_ref[...]
    xn = _rms(x, g_ref[...]).astype(bf16)
    d = None
    for c in range(D_FF // FF_CHUNK):
        cs = slice(c * FF_CHUNK, (c + 1) * FF_CHUNK)
        gate = jnp.dot(xn, wg_ref[:, cs], preferred_element_type=f32)
        up = jnp.dot(xn, wu_ref[:, cs], preferred_element_type=f32)
        h = (gate * jax.nn.sigmoid(gate) * up).astype(bf16)
        dc = jnp.dot(h, wd_ref[cs, :], preferred_element_type=f32)
        d = dc if d is None else d + dc
    o_ref[...] = x + 0.5 * d


def _ffn(x, g, wg, wu, wd, l, s):
    n = x.shape[0]
    return pl.pallas_call(
        _ffn_body,
        grid=(n // TM,),
        in_specs=[
            pl.BlockSpec((TM, D_MODEL), lambda i: (i, 0)),
            _layer_spec((1, D_MODEL), 3 * l + 2 * s),
            _layer_spec((D_MODEL, D_FF), 2 * l + s),
            _layer_spec((D_MODEL, D_FF), 2 * l + s),
            _layer_spec((D_FF, D_MODEL), 2 * l + s),
        ],
        out_specs=pl.BlockSpec((TM, D_MODEL), lambda i: (i, 0)),
        out_shape=jax.ShapeDtypeStruct((n, D_MODEL), f32),
        compiler_params=_cparams("parallel"),
        name="ffn",
    )(x, g, wg, wu, wd)


def _att_out_body(x_ref, y_ref, w_ref, o_ref):
    o_ref[...] = x_ref[...] + jnp.dot(y_ref[...].astype(bf16), w_ref[...], preferred_element_type=f32)


def _att_out(x, y, w, a):
    n = x.shape[0]
    return pl.pallas_call(
        _att_out_body,
        grid=(n // TM,),
        in_specs=[
            pl.BlockSpec((TM, D_MODEL), lambda i: (i, 0)),
            pl.BlockSpec((TM, D_MODEL), lambda i: (i, 0)),
            _layer_spec((D_MODEL, D_MODEL), a),
        ],
        out_specs=pl.BlockSpec((TM, D_MODEL), lambda i: (i, 0)),
        out_shape=jax.ShapeDtypeStruct((n, D_MODEL), f32),
        compiler_params=_cparams("parallel"),
        name="att_out",
    )(x, y, w)


def _rec_out_body(x_ref, y_ref, gw_ref, gb_ref, w_ref, o_ref):
    y = y_ref[...]
    ya = y[:, :W_LRU].astype(bf16)
    g5 = y[:, W_LRU:]
    glu = jax.nn.sigmoid(jnp.dot(g5.astype(bf16), gw_ref[...], preferred_element_type=f32) + gb_ref[...])
    yb = (g5 * glu).astype(bf16)
    yy = jnp.concatenate([ya, yb], axis=-1)
    o_ref[...] = x_ref[...] + jnp.dot(yy, w_ref[...], preferred_element_type=f32)


def _rec_out(x, y, glu_w, glu_b, w_out, r):
    n = x.shape[0]
    return pl.pallas_call(
        _rec_out_body,
        grid=(n // TM,),
        in_specs=[
            pl.BlockSpec((TM, D_MODEL), lambda i: (i, 0)),
            pl.BlockSpec((TM, D_MODEL), lambda i: (i, 0)),
            _layer_spec((W_S5, W_S5), r),
            _layer_spec((1, W_S5), r),
            _layer_spec((D_MODEL, D_MODEL), r),
        ],
        out_specs=pl.BlockSpec((TM, D_MODEL), lambda i: (i, 0)),
        out_shape=jax.ShapeDtypeStruct((n, D_MODEL), f32),
        compiler_params=_cparams("parallel"),
        name="rec_out",
    )(x, y, glu_w, glu_b, w_out)


def _rec_scan_body(bsz, tc, x_ref, g_ref, win_ref, conv0_ref, lru0_ref, s5r0_ref, s5i0_ref, cw_ref, cb_ref, wr_ref,
                   br_ref, wi_ref, bi_ref, lam_ref, ar_ref, ai_ref, bbr_ref, bbi_ref, ccr_ref, cci_ref, d_ref,
                   y_ref, conv_out, lru_out, s5r_out, s5i_out,
                   z_ref, xbuf, a_scr, u_scr, hr_scr, hi_scr, lru_c, s5r_c, s5i_c):
    rows = bsz * tc
    step = pl.program_id(0)
    last = pl.num_programs(0) - 1
    z_ref[...] = jnp.dot(_rms(x_ref[...], g_ref[...]).astype(bf16), win_ref[...], preferred_element_type=f32)

    @pl.when(step == 0)
    def _():
        xbuf[0:3 * bsz, :] = conv0_ref[...]
        lru_c[...] = lru0_ref[...]
        s5r_c[...] = s5r0_ref[...]
        s5i_c[...] = s5i0_ref[...]

    xbuf[3 * bsz:, :] = z_ref[:, 0:W_LRU]
    xc = cb_ref[...] + cw_ref[0:1, :] * xbuf[0:rows, :]
    for j in range(1, CONV_W):
        xc = xc + cw_ref[j:j + 1, :] * xbuf[j * bsz:j * bsz + rows, :]
    tail = xbuf[rows:rows + 3 * bsz, :]

    xcb = xc.astype(bf16)
    r = jax.nn.sigmoid(jnp.dot(xcb, wr_ref[...], preferred_element_type=f32) + br_ref[...])
    ig = jax.nn.sigmoid(jnp.dot(xcb, wi_ref[...], preferred_element_type=f32) + bi_ref[...])
    lam = lam_ref[...]
    log_sig = jnp.minimum(lam, 0.0) - jnp.log1p(jnp.exp(-jnp.abs(lam)))
    log_a = (LRU_C * r) * log_sig
    a = jnp.exp(log_a)
    mult = jnp.sqrt(-jnp.tanh(log_a) * (a * a + 1.0))
    a_scr[...] = a
    u_scr[...] = mult * (ig * xc)

    def lru_step(t, h):
        rs = pl.ds(pl.multiple_of(t * bsz, 8), bsz)
        h = a_scr[rs, :] * h + u_scr[rs, :]
        u_scr[rs, :] = h
        return h

    h_last = lax.fori_loop(0, tc, lru_step, lru_c[...], unroll=True)
    lru_c[...] = h_last
    y_ref[:, 0:W_LRU] = u_scr[...] * jax.nn.gelu(z_ref[:, W_LRU:2 * W_LRU])

    half_in, half_st = W_S5 // 2, S5_N // 2
    for k in range(2):
        ubk = z_ref[:, 2 * W_LRU + k * half_in:2 * W_LRU + (k + 1) * half_in].astype(bf16)
        hr_scr[:, k * half_st:(k + 1) * half_st] = jnp.dot(ubk, bbr_ref[k], preferred_element_type=f32)
        hi_scr[:, k * half_st:(k + 1) * half_st] = jnp.dot(ubk, bbi_ref[k], preferred_element_type=f32)

    lane_chunk = 512
    for c in range(S5_N // lane_chunk):
        cs = slice(c * lane_chunk, (c + 1) * lane_chunk)
        ar = jnp.broadcast_to(ar_ref[:, cs], (bsz, lane_chunk))
        ai = jnp.broadcast_to(ai_ref[:, cs], (bsz, lane_chunk))

        def s5_step(t, carry, cs=cs, ar=ar, ai=ai):
            hr, hi = carry
            rs = pl.ds(pl.multiple_of(t * bsz, 8), bsz)
            nr = ar * hr - ai * hi + hr_scr[rs, cs]
            ni = ar * hi + ai * hr + hi_scr[rs, cs]
            hr_scr[rs, cs] = nr
            hi_scr[rs, cs] = ni
            return nr, ni

        hr, hi = lax.fori_loop(0, tc, s5_step, (s5r_c[:, cs], s5i_c[:, cs]), unroll=True)
        s5r_c[:, cs] = hr
        s5i_c[:, cs] = hi

    for k in range(2):
        hrk = hr_scr[:, k * half_st:(k + 1) * half_st].astype(bf16)
        hik = hi_scr[:, k * half_st:(k + 1) * half_st].astype(bf16)
        ys = (jnp.dot(hrk, ccr_ref[k], preferred_element_type=f32)
              - jnp.dot(hik, cci_ref[k], preferred_element_type=f32))
        ub = z_ref[:, 2 * W_LRU + k * half_in:2 * W_LRU + (k + 1) * half_in]
        ys = ys + d_ref[:, k * half_in:(k + 1) * half_in] * ub
        y_ref[:, W_LRU + k * half_in:W_LRU + (k + 1) * half_in] = jax.nn.gelu(ys)

    xbuf[0:3 * bsz, :] = tail

    @pl.when(step == last)
    def _():
        conv_out[...] = tail
        lru_out[...] = lru_c[...]
        s5r_out[...] = s5r_c[...]
        s5i_out[...] = s5i_c[...]


def _rec_scan(x, bsz, tc, g, gi, w_in, conv0, lru0, s5r0, s5i0, prm, r):
    n = x.shape[0]
    rows = bsz * tc
    full = lambda shape: pl.BlockSpec(shape, lambda i: (0,) * len(shape))
    in_specs = [
        pl.BlockSpec((rows, D_MODEL), lambda i: (i, 0)),
        _layer_spec((1, D_MODEL), gi),
        _layer_spec((D_MODEL, 3 * W_LRU), r),
        full((3 * bsz, W_LRU)), full((bsz, W_LRU)), full((bsz, S5_N)), full((bsz, S5_N)),
    ] + [_layer_spec(p.shape[1:], r) for p in prm]
    out_shape = (
        jax.ShapeDtypeStruct((n, D_MODEL), f32),
        jax.ShapeDtypeStruct((3 * bsz, W_LRU), f32),
        jax.ShapeDtypeStruct((bsz, W_LRU), f32),
        jax.ShapeDtypeStruct((bsz, S5_N), f32),
        jax.ShapeDtypeStruct((bsz, S5_N), f32),
    )
    out_specs = (
        pl.BlockSpec((rows, D_MODEL), lambda i: (i, 0)),
        full((3 * bsz, W_LRU)), full((bsz, W_LRU)), full((bsz, S5_N)), full((bsz, S5_N)),
    )
    scratch = [
        pltpu.VMEM((rows, 3 * W_LRU), f32),
        pltpu.VMEM((rows + 3 * bsz, W_LRU), f32),
        pltpu.VMEM((rows, W_LRU), f32), pltpu.VMEM((rows, W_LRU), f32),
        pltpu.VMEM((rows, S5_N), f32), pltpu.VMEM((rows, S5_N), f32),
        pltpu.VMEM((bsz, W_LRU), f32), pltpu.VMEM((bsz, S5_N), f32), pltpu.VMEM((bsz, S5_N), f32),
    ]
    return pl.pallas_call(
        functools.partial(_rec_scan_body, bsz, tc),
        grid=(n // rows,),
        in_specs=in_specs,
        out_specs=out_specs,
        out_shape=out_shape,
        scratch_shapes=scratch,
        compiler_params=_cparams("arbitrary"),
        name="rec_scan",
    )(x, g, w_in, conv0, lru0, s5r0, s5i0, *prm)


def _head_norm_t(t, gain):
    n = t.shape[1]
    t3 = t.reshape(N_HEADS, HEAD_DIM, n)
    ms = jnp.mean(t3 * t3, axis=1, keepdims=True)
    return (t3 * lax.rsqrt(ms + EPS) * gain[None]).reshape(N_HEADS * HEAD_DIM, n)


def _qkv_body(n_alias, x_ref, g_ref, wt_ref, qg_ref, kg_ref, *refs):
    qb_ref, qf_ref, kt_ref, vt_ref, km_ref = refs[n_alias:]
    xn = _rms(x_ref[...], g_ref[...])
    xnt = xn.T.astype(bf16)
    qkvt = jnp.dot(wt_ref[...], xnt, preferred_element_type=f32)
    qt = _head_norm_t(qkvt[0:D_MODEL], qg_ref[...])
    kt = _head_norm_t(qkvt[D_MODEL:2 * D_MODEL], kg_ref[...])
    q = qt.T
    qf_ref[...] = q
    qb_ref[...] = (q * (HEAD_DIM ** -0.5)).astype(bf16)
    if n_alias:
        kt_ref[0] = kt
        vt_ref[0] = qkvt[2 * D_MODEL:]
    else:
        for j in range(kt_ref.shape[0]):
            kt_ref[j, 0] = kt
            vt_ref[j, 0] = qkvt[2 * D_MODEL:]
    for n in range(TM // MOBA_BLOCK):
        km_ref[0, :, n:n + 1] = jnp.mean(kt[:, n * MOBA_BLOCK:(n + 1) * MOBA_BLOCK], axis=1, keepdims=True)


def _qkv(x, g, gi, wt, qg, kg, a, nb, kv_prev=None):
    n = x.shape[0]
    tlen = n // nb
    tpb = tlen // TM
    n_att = wt.shape[0]
    prev = () if kv_prev is None else tuple(kv_prev)
    if prev:
        kv_spec = pl.BlockSpec((None, 1, D_MODEL, TM), lambda i: (a, i // tpb, 0, i % tpb))
    else:
        kv_spec = pl.BlockSpec((n_att, 1, D_MODEL, TM), lambda i: (0, i // tpb, 0, i % tpb))
    return pl.pallas_call(
        functools.partial(_qkv_body, len(prev)),
        grid=(n // TM,),
        in_specs=[
            pl.BlockSpec((TM, D_MODEL), lambda i: (i, 0)),
            _layer_spec((1, D_MODEL), gi),
            _layer_spec((3 * D_MODEL, D_MODEL), a),
            _layer_spec((HEAD_DIM, TM), a),
            _layer_spec((HEAD_DIM, TM), a),
        ] + [pl.BlockSpec(memory_space=pl.ANY)] * len(prev),
        out_specs=(
            pl.BlockSpec((TM, D_MODEL), lambda i: (i, 0)),
            pl.BlockSpec((TM, D_MODEL), lambda i: (i, 0)),
            kv_spec,
            kv_spec,
            pl.BlockSpec((1, D_MODEL, TM // MOBA_BLOCK), lambda i: (i, 0, 0)),
        ),
        out_shape=(
            jax.ShapeDtypeStruct((n, D_MODEL), bf16),
            jax.ShapeDtypeStruct((n, D_MODEL), f32),
            jax.ShapeDtypeStruct((n_att, nb, D_MODEL, tlen), f32),
            jax.ShapeDtypeStruct((n_att, nb, D_MODEL, tlen), f32),
            jax.ShapeDtypeStruct((n // TM, D_MODEL, TM // MOBA_BLOCK), f32),
        ),
        input_output_aliases={5 + j: 2 + j for j in range(len(prev))},
        compiler_params=_cparams("parallel"),
        name="qkv",
    )(x, g, wt, qg, kg, *prev)


def _split_bf16(x):
    hi = x.astype(bf16)
    lo = (x - hi.astype(f32)).astype(bf16)
    return hi, lo


def _attn_prompt_body(nblk, hg, nqb, qb_ref, qf_ref, kt_ref, vt_ref, km_ref, bias_ref, o_ref,
                      ka_scr, va_scr, qa_scr, s_scr, p_scr):
    step = pl.program_id(2)
    pw = 2 * HEAD_DIM
    tlen = kt_ref.shape[2]
    qrows = nqb * MOBA_BLOCK
    lane = lax.broadcasted_iota(jnp.int32, (1, pw), 1)
    nio = lax.broadcasted_iota(jnp.int32, (nblk, MOBA_BLOCK), 0)

    @pl.when(step == 0)
    def _():
        blk_rows = lax.broadcasted_iota(jnp.int32, (pw, tlen), 0)
        blk_cols = lax.broadcasted_iota(jnp.int32, (pw, tlen), 1) // MOBA_BLOCK
        onehot = jnp.where((blk_rows == blk_cols) | (blk_rows == blk_cols + nblk), 1.0, 0.0).astype(bf16)
        for pp in range(hg // 2):
            ka_scr[pp, 0:pw, :] = kt_ref[0, pp * pw:(pp + 1) * pw, :].astype(bf16)
            ka_scr[pp, pw:2 * pw, :] = onehot
            va_scr[pp, 0:pw, :] = vt_ref[0, pp * pw:(pp + 1) * pw, :].astype(bf16)
            va_scr[pp, pw:2 * pw, :] = jnp.ones((pw, tlen), bf16)

    for h in range(hg):
        ps = slice((h // 2) * pw, (h // 2 + 1) * pw)
        hmask = (lane // HEAD_DIM) == h % 2
        k_hi, k_lo = _split_bf16(km_ref[0, 0, :, ps])
        b_far = bias_ref[0, h, 2, 0:1, 0:1]
        b_hi = b_far.astype(bf16).astype(f32)
        for qi in range(nqb):
            i = step * nqb + qi
            rows = slice(qi * MOBA_BLOCK, (qi + 1) * MOBA_BLOCK)
            qa_scr[h, rows, 0:pw] = jnp.where(hmask, qb_ref[rows, ps], jnp.zeros((), bf16))
            q_hi, q_lo = _split_bf16(jnp.where(hmask, qf_ref[rows, ps], 0.0))
            gate = (lax.dot_general(k_hi, q_hi, _DN_T, preferred_element_type=f32)
                    + lax.dot_general(k_lo, q_hi, _DN_T, preferred_element_type=f32)
                    + lax.dot_general(k_hi, q_lo, _DN_T, preferred_element_type=f32))[0:nblk]
            rank = jnp.zeros((nblk, MOBA_BLOCK), f32)
            for m in range(nblk):
                gm = gate[m:m + 1, :]
                beats = jnp.where(gm > gate, 1.0, jnp.where(gm == gate, jnp.where(m < nio, 1.0, 0.0), 0.0))
                rank = rank + jnp.where(m < i, beats, 0.0)
            keep = jnp.where(nio < i, jnp.where(rank < MOBA_TOPK, 1.0, 0.0), jnp.where(nio == i, 1.0, 0.0)) > 0.5
            far = nio < i - 1
            term_hi = jnp.where(keep, jnp.where(far, b_hi, 0.0), NEG)
            term_lo = jnp.where(keep & far, b_far - b_hi, 0.0)
            terms = jnp.concatenate([term_hi, term_lo, jnp.zeros((pw - 2 * nblk, MOBA_BLOCK), f32)], axis=0).T
            qa_scr[h, rows, pw:2 * pw] = terms.astype(bf16)

    nrow = 2 * qrows

    def process(width):
        def scores(h):
            hrows = slice((h % 2) * qrows, (h % 2 + 1) * qrows)
            s_scr[h // 2, hrows, 0:width] = jnp.dot(qa_scr[h], ka_scr[h // 2, :, 0:width], preferred_element_type=f32)
            for qi in range(nqb):
                i = step * nqb + qi
                rows = slice((h % 2) * qrows + qi * MOBA_BLOCK, (h % 2) * qrows + (qi + 1) * MOBA_BLOCK)
                near = pl.multiple_of(i * MOBA_BLOCK, MOBA_BLOCK)
                prev = pl.multiple_of(jnp.maximum(i - 1, 0) * MOBA_BLOCK, MOBA_BLOCK)
                prev_tile = jnp.where(i >= 1, 1, 3)
                s_scr[h // 2, rows, pl.ds(near, MOBA_BLOCK)] += bias_ref[0, h, 0]
                s_scr[h // 2, rows, pl.ds(prev, MOBA_BLOCK)] += bias_ref[0, h, prev_tile]
        def softmax_pv(h):
            pp, base = h // 2, (h % 2) * qrows
            for r in range(qrows // ATT_ROW_CHUNK):
                rs = slice(base + r * ATT_ROW_CHUNK, base + (r + 1) * ATT_ROW_CHUNK)
                mx = s_scr[pp, rs, 0:MOBA_BLOCK]
                for t in range(1, width // MOBA_BLOCK):
                    mx = jnp.maximum(mx, s_scr[pp, rs, t * MOBA_BLOCK:(t + 1) * MOBA_BLOCK])
                mx = jnp.max(mx, axis=-1, keepdims=True)
                for t in range(width // MOBA_BLOCK):
                    cs = slice(t * MOBA_BLOCK, (t + 1) * MOBA_BLOCK)
                    p_scr[pp, rs, cs] = jnp.exp(s_scr[pp, rs, cs] - mx).astype(bf16)
            o = lax.dot_general(p_scr[pp, base:base + qrows, 0:width], va_scr[pp, :, 0:width], _DN_T,
                                preferred_element_type=f32)
            return o[:, 0:pw] / o[:, pw:2 * pw]

        for h in range(hg):
            scores(h)
        res = [softmax_pv(h) for h in range(hg)]
        for pp in range(hg // 2):
            o_ref[:, pp * pw:(pp + 1) * pw] = jnp.where(lane < HEAD_DIM, res[2 * pp], res[2 * pp + 1]).astype(o_ref.dtype)

    for k in range(nblk // nqb):
        @pl.when(step == k)
        def _(k=k):
            process((k + 1) * qrows)


def _attn_prompt(qb, qf, kt, vt, a, km, bias):
    _, nb, _, tlen = kt.shape
    nq = tlen // MOBA_BLOCK
    hg, nqb = ATT_HEAD_GROUP, ATT_Q_BLOCKS
    gw = hg * HEAD_DIM
    qrows = nqb * MOBA_BLOCK
    steps = nq // nqb
    kv_spec = pl.BlockSpec((None, 1, gw, tlen), lambda p, b, i: (a, b, p, 0))
    return pl.pallas_call(
        functools.partial(_attn_prompt_body, nq, hg, nqb),
        grid=(N_HEADS // hg, nb, steps),
        in_specs=[
            pl.BlockSpec((qrows, gw), lambda p, b, i: (b * steps + i, p)),
            pl.BlockSpec((qrows, gw), lambda p, b, i: (b * steps + i, p)),
            kv_spec,
            kv_spec,
            pl.BlockSpec((1, 1, km.shape[2], gw), lambda p, b, i: (b, p, 0, 0)),
            pl.BlockSpec((1, hg, 4, MOBA_BLOCK, MOBA_BLOCK), lambda p, b, i: (p, 0, 0, 0, 0)),
        ],
        out_specs=pl.BlockSpec((qrows, gw), lambda p, b, i: (b * steps + i, p)),
        out_shape=jax.ShapeDtypeStruct((nb * tlen, D_MODEL), bf16),
        scratch_shapes=[
            pltpu.VMEM((hg // 2, 4 * HEAD_DIM, tlen), bf16),
            pltpu.VMEM((hg // 2, 4 * HEAD_DIM, tlen), bf16),
            pltpu.VMEM((hg, qrows, 4 * HEAD_DIM), bf16),
            pltpu.VMEM((hg // 2, 2 * qrows, tlen), f32),
            pltpu.VMEM((hg // 2, 2 * qrows, tlen), bf16),
        ],
        compiler_params=_cparams("parallel", "parallel", "arbitrary"),
        name="attn_prompt",
    )(qb, qf, kt, vt, km, bias)


def _attn_sample_body(layer, n_pages, pt_ref, q_ref, kn_ref, vn_ref, bias_ref, ck_hbm, cv_hbm, o_ref,
                      kbuf, vbuf, sems, s_scr, p_scr):
    b = pl.program_id(0)
    slot = b % 2
    nq = q_ref.shape[1]
    npg = n_pages + 1
    rows = N_HEADS * 8

    def k_copy(bb, sl, p):
        return pltpu.make_async_copy(ck_hbm.at[layer, pt_ref[bb, p]], kbuf.at[sl, p], sems.at[0, sl, p])

    def v_copy(bb, sl, p):
        return pltpu.make_async_copy(cv_hbm.at[layer, pt_ref[bb, p]], vbuf.at[sl, p], sems.at[1, sl, p])

    def start_all(bb, sl):
        for p in range(n_pages):
            k_copy(bb, sl, p).start()
        for p in range(n_pages):
            v_copy(bb, sl, p).start()

    @pl.when(b == 0)
    def _():
        start_all(0, 0)

    @pl.when(b + 1 < pl.num_programs(0))
    def _():
        start_all(b + 1, 1 - slot)

    q8 = jnp.concatenate([q_ref[0] * (HEAD_DIM ** -0.5), jnp.zeros((8 - nq, D_MODEL), f32)], axis=0)
    rh = lax.broadcasted_iota(jnp.int32, (rows, D_MODEL), 0) // 8
    ch = lax.broadcasted_iota(jnp.int32, (rows, D_MODEL), 1) // HEAD_DIM
    qblk = jnp.where(rh == ch, jnp.tile(q8, (N_HEADS, 1)), 0.0).astype(bf16)

    def page_scores(kpage):
        kp = kpage.reshape(N_HEADS * HEAD_DIM, PAGE_SIZE).astype(bf16)
        return jnp.dot(qblk, kp, preferred_element_type=f32).reshape(N_HEADS, 8, PAGE_SIZE)

    for p in range(n_pages):
        k_copy(b, slot, p).wait()
    for p in range(n_pages):
        s_scr[p] = page_scores(kbuf[slot, p])

    def new_page(ref):
        return jnp.concatenate([ref[0], jnp.zeros((PAGE_SIZE - nq, D_MODEL), f32)], axis=0).astype(bf16)

    s_scr[n_pages] = lax.dot_general(qblk, new_page(kn_ref), _DN_T,
                                     preferred_element_type=f32).reshape(N_HEADS, 8, PAGE_SIZE)

    ppb = MOBA_BLOCK // PAGE_SIZE
    nblk = n_pages // ppb
    gates = []
    for n in range(nblk):
        blk = s_scr[n * ppb]
        for e in range(1, ppb):
            blk = blk + s_scr[n * ppb + e]
        gates.append(jnp.sum(blk, axis=-1, keepdims=True))
    sel = []
    for n in range(nblk):
        rank = jnp.zeros_like(gates[n])
        for m in range(nblk):
            if m == n:
                continue
            if m < n:
                rank = rank + jnp.where(gates[m] >= gates[n], 1.0, 0.0)
            else:
                rank = rank + jnp.where(gates[m] > gates[n], 1.0, 0.0)
        sel.append(rank < MOBA_TOPK)

    mx_e = jnp.full((N_HEADS, 8, PAGE_SIZE), -1e38, f32)
    for p in range(npg):
        s = s_scr[p] + bias_ref[p]
        if p < n_pages:
            s = jnp.where(sel[p // ppb], s, NEG)
        s_scr[p] = s
        mx_e = jnp.maximum(mx_e, s)
    mx = jnp.max(mx_e, axis=-1, keepdims=True)
    den_e = jnp.zeros((N_HEADS, 8, PAGE_SIZE), f32)
    for p in range(npg):
        e = jnp.exp(s_scr[p] - mx)
        p_scr[p] = e
        den_e = den_e + e
    inv = 1.0 / jnp.sum(den_e, axis=-1, keepdims=True)

    for p in range(n_pages):
        v_copy(b, slot, p).wait()

    def page_pv(p, vpage):
        pr = (p_scr[p] * inv).reshape(rows, PAGE_SIZE).astype(bf16)
        vp = vpage.reshape(N_HEADS * HEAD_DIM, PAGE_SIZE).astype(bf16)
        return lax.dot_general(pr, vp, _DN_T, preferred_element_type=f32)

    pr_new = (p_scr[n_pages] * inv).reshape(rows, PAGE_SIZE).astype(bf16)
    acc = jnp.dot(pr_new, new_page(vn_ref), preferred_element_type=f32)
    for p in range(n_pages):
        acc = acc + page_pv(p, vbuf[slot, p])
    lane_head = lax.broadcasted_iota(jnp.int32, (8, D_MODEL), 1) // HEAD_DIM
    out = jnp.zeros((8, D_MODEL), f32)
    for h in range(N_HEADS):
        out = out + jnp.where(lane_head == h, acc[h * 8:(h + 1) * 8, :], 0.0)
    o_ref[0] = out[0:nq, :].astype(o_ref.dtype)


def _attn_sample(layer, page_table, q, kn, vn, bias, ck, cv):
    nb, nq, _ = q.shape
    n_pages = page_table.shape[1]
    npg = n_pages + 1
    grid_spec = pltpu.PrefetchScalarGridSpec(
        num_scalar_prefetch=1,
        grid=(nb,),
        in_specs=[
            pl.BlockSpec((1, nq, D_MODEL), lambda b, pt: (b, 0, 0)),
            pl.BlockSpec((1, nq, D_MODEL), lambda b, pt: (b, 0, 0)),
            pl.BlockSpec((1, nq, D_MODEL), lambda b, pt: (b, 0, 0)),
            pl.BlockSpec((npg, N_HEADS, 8, PAGE_SIZE), lambda b, pt: (0, 0, 0, 0)),
            pl.BlockSpec(memory_space=pl.ANY),
            pl.BlockSpec(memory_space=pl.ANY),
        ],
        out_specs=pl.BlockSpec((1, nq, D_MODEL), lambda b, pt: (b, 0, 0)),
        scratch_shapes=[
            pltpu.VMEM((2, n_pages, N_HEADS, HEAD_DIM, PAGE_SIZE), f32),
            pltpu.VMEM((2, n_pages, N_HEADS, HEAD_DIM, PAGE_SIZE), f32),
            pltpu.SemaphoreType.DMA((2, 2, n_pages)),
            pltpu.VMEM((npg, N_HEADS, 8, PAGE_SIZE), f32),
            pltpu.VMEM((npg, N_HEADS, 8, PAGE_SIZE), f32),
        ],
    )
    return pl.pallas_call(
        functools.partial(_attn_sample_body, layer, n_pages),
        grid_spec=grid_spec,
        out_shape=jax.ShapeDtypeStruct((nb, nq, D_MODEL), f32),
        compiler_params=_cparams("arbitrary"),
        name="attn_sample",
    )(page_table, q, kn, vn, bias, ck, cv)


def _t5_bucket_np(dist):
    dist = np.asarray(dist)
    max_exact = REL_BUCKETS // 2
    log_ratio = (np.log(np.maximum(dist, 1).astype(np.float32) / np.float32(max_exact))
                 / np.float32(math.log(REL_MAX_DIST / max_exact))).astype(np.float32)
    large = np.minimum(max_exact + (log_ratio * np.float32(REL_BUCKETS - max_exact)).astype(np.int32), REL_BUCKETS - 1)
    return np.where(dist < max_exact, dist, large)


def _prompt_bias(rel_bias):
    blk, span = MOBA_BLOCK, 2 * MOBA_BLOCK
    by_dist = rel_bias[_t5_bucket_np(np.arange(3 * blk))].T
    j = np.arange(span)
    j = np.where(j < blk, j, j - span)
    tiles = []
    for d in range(3):
        dist = d * blk - j
        w = jnp.where((dist >= 0)[None, :], by_dist[:, np.clip(dist, 0, 3 * blk - 1)], NEG)
        t = jnp.tile(w, (1, blk))[:, :blk * (span - 1)].reshape(N_HEADS, blk, span - 1)
        tiles.append(t[:, :, :blk])
    tiles.append(jnp.zeros_like(tiles[0]))
    t = jnp.stack(tiles, axis=1)
    return t.reshape(N_HEADS // ATT_HEAD_GROUP, ATT_HEAD_GROUP, 4, blk, blk)


def _group_block_means(km, nb, tlen):
    gw = ATT_HEAD_GROUP * HEAD_DIM
    nblk = tlen // MOBA_BLOCK
    km = jnp.transpose(km.reshape(nb, tlen // TM, N_HEADS // ATT_HEAD_GROUP, gw, TM // MOBA_BLOCK), (0, 2, 1, 4, 3))
    km = km.reshape(nb, N_HEADS // ATT_HEAD_GROUP, nblk, gw)
    return jnp.pad(km, ((0, 0), (0, 0), (0, 16 - nblk), (0, 0)))


def _sample_bias(rel_bias, n_pages, nq):
    past = n_pages * PAGE_SIZE
    nkeys = (n_pages + 1) * PAGE_SIZE
    span = past + 8
    by_dist = rel_bias[_t5_bucket_np(np.arange(span))].T
    rev = jnp.pad(by_dist[:, ::-1], ((0, 0), (0, nkeys)))
    rows = jnp.stack([rev[:, 7 - t:7 - t + nkeys] for t in range(8)], axis=1)
    kpos = np.arange(nkeys)
    t8 = np.arange(8)[:, None]
    valid = (past + t8 - kpos[None, :] >= 0) & (t8 < nq) & (kpos[None, :] < past + nq)
    fill = np.where(t8 < nq, NEG, 0.0).astype(np.float32)
    b = jnp.where(valid[None], rows, fill[None])
    return jnp.transpose(b.reshape(N_HEADS, 8, n_pages + 1, PAGE_SIZE), (2, 0, 1, 3))


def _block_diag(w):
    g, a, b = w.shape
    eye = jnp.eye(g, dtype=w.dtype)
    return jnp.einsum('gab,gh->gahb', w, eye).reshape(g * a, g * b)


def _rec_params(conv_w, conv_b, w_r, b_r, w_i, b_i, lam, lam_re, lam_im, log_step, b_re, b_im, c_re, c_im, d):
    dt = jnp.exp(log_step)[:, None]
    mag = jnp.exp(lam_re * dt)
    ar, ai = mag * jnp.cos(lam_im * dt), mag * jnp.sin(lam_im * dt)
    den = lam_re * lam_re + lam_im * lam_im
    qr = ((ar - 1.0) * lam_re + ai * lam_im) / den
    qi = (ai * lam_re - (ar - 1.0) * lam_im) / den
    bbr = qr[..., None] * b_re - qi[..., None] * b_im
    bbi = qr[..., None] * b_im + qi[..., None] * b_re

    def halves(m):
        return jnp.stack([m[:m.shape[0] // 2, :m.shape[1] // 2], m[m.shape[0] // 2:, m.shape[1] // 2:]]).astype(bf16)

    bbr_m = halves(_block_diag(jnp.transpose(bbr, (0, 2, 1))))
    bbi_m = halves(_block_diag(jnp.transpose(bbi, (0, 2, 1))))
    ccr_m = halves(_block_diag(jnp.transpose(c_re, (0, 2, 1))))
    cci_m = halves(_block_diag(jnp.transpose(c_im, (0, 2, 1))))
    return (conv_w, conv_b[None], _block_diag(w_r).astype(bf16), b_r[None], _block_diag(w_i).astype(bf16), b_i[None],
            lam[None], ar.reshape(1, S5_N), ai.reshape(1, S5_N), bbr_m, bbi_m, ccr_m, cci_m, d[None])


def _to_time_major(x):
    b, t, c = x.shape
    return jnp.transpose(x, (1, 0, 2)).reshape(t * b, c)


def _from_time_major(y, b):
    n, c = y.shape
    return jnp.transpose(y.reshape(n // b, b, c), (1, 0, 2)).reshape(n, c)


def _recurrent_layer(x2d, bsz, tc, g, gi, w_in, conv0, lru0, s5r0, s5i0, prm, glu_w, glu_b, w_out, r):
    n = x2d.shape[0]
    x_tb = _to_time_major(x2d.reshape(bsz, n // bsz, D_MODEL))
    y_tb, conv_n, lru_n, s5r_n, s5i_n = _rec_scan(x_tb, bsz, tc, g, gi, w_in, conv0, lru0, s5r0, s5i0, prm, r)
    y = _from_time_major(y_tb, bsz)
    return _rec_out(x2d, y, glu_w, glu_b, w_out, r), conv_n, lru_n, s5r_n, s5i_n


def kernel(x_prompt, x_sample, cache_k, cache_v, state_lru_h, state_conv, state_s5_re, state_s5_im, page_table, norm_g, ffn_w_gate, ffn_w_up, ffn_w_down, rec_w_in, rec_conv_w, rec_conv_b, lru_w_r, lru_b_r, lru_w_i, lru_b_i, lru_lambda, s5_lambda_re, s5_lambda_im, s5_log_step, s5_b_re, s5_b_im, s5_c_re, s5_c_im, s5_d, s5_glu_w, s5_glu_b, rec_w_out, att_w_qkv, att_q_norm, att_k_norm, att_w_o, rel_bias):
    bp, tp, _ = x_prompt.shape
    bs, ts, _ = x_sample.shape
    depth = norm_g.shape[0]
    n_pages = page_table.shape[1]
    xp = x_prompt.reshape(bp * tp, D_MODEL)
    xs = x_sample.reshape(bs * ts, D_MODEL)

    g_all = norm_g.reshape(depth * 3, 1, D_MODEL)
    wg_all = ffn_w_gate.astype(bf16).reshape(depth * 2, D_MODEL, D_FF)
    wu_all = ffn_w_up.astype(bf16).reshape(depth * 2, D_MODEL, D_FF)
    wd_all = ffn_w_down.astype(bf16).reshape(depth * 2, D_FF, D_MODEL)
    w_in_all = rec_w_in.astype(bf16)
    glu_w_all = s5_glu_w.astype(bf16)
    glu_b_all = s5_glu_b[:, None, :]
    w_out_all = rec_w_out.astype(bf16)
    rec_prm = jax.vmap(_rec_params)(rec_conv_w, rec_conv_b, lru_w_r, lru_b_r, lru_w_i, lru_b_i, lru_lambda,
                                    s5_lambda_re, s5_lambda_im, s5_log_step, s5_b_re, s5_b_im, s5_c_re, s5_c_im, s5_d)
    wt_all = jnp.transpose(att_w_qkv, (0, 2, 1)).astype(bf16)
    qg_all = jnp.broadcast_to(att_q_norm[:, :, None], att_q_norm.shape + (TM,))
    kg_all = jnp.broadcast_to(att_k_norm[:, :, None], att_k_norm.shape + (TM,))
    w_o_all = att_w_o.astype(bf16)
    ckt = jnp.transpose(cache_k, (0, 1, 3, 4, 2))
    cvt = jnp.transpose(cache_v, (0, 1, 3, 4, 2))
    bias_p = _prompt_bias(rel_bias)
    bias_s = _sample_bias(rel_bias, n_pages, ts)

    kv_p = kv_s = None
    lru_p, lru_s, conv_p, conv_s, s5r_p, s5r_s, s5i_p, s5i_s = [], [], [], [], [], [], [], []
    for l in range(depth):
        xp = _ffn(xp, g_all, wg_all, wu_all, wd_all, l, 0)
        xs = _ffn(xs, g_all, wg_all, wu_all, wd_all, l, 0)
        gi = 3 * l + 1
        if l % 2 == 0:
            r = l // 2
            xp, cp, lp, srp, sip = _recurrent_layer(
                xp, bp, REC_TIME_CHUNK, g_all, gi, w_in_all, jnp.zeros((3 * bp, W_LRU), f32),
                jnp.zeros((bp, W_LRU), f32), jnp.zeros((bp, S5_N), f32), jnp.zeros((bp, S5_N), f32),
                rec_prm, glu_w_all, glu_b_all, w_out_all, r)
            conv0 = jnp.transpose(state_conv[r], (1, 0, 2)).reshape(3 * bs, W_LRU)
            xs, cs, ls, srs, sis = _recurrent_layer(
                xs, bs, ts, g_all, gi, w_in_all, conv0, state_lru_h[r], state_s5_re[r].reshape(bs, S5_N),
                state_s5_im[r].reshape(bs, S5_N), rec_prm, glu_w_all, glu_b_all, w_out_all, r)
            conv_p.append(jnp.transpose(cp.reshape(3, bp, W_LRU), (1, 0, 2)))
            conv_s.append(jnp.transpose(cs.reshape(3, bs, W_LRU), (1, 0, 2)))
            lru_p.append(lp); lru_s.append(ls)
            s5r_p.append(srp.reshape(bp, S5_GROUPS, S5_STATE)); s5r_s.append(srs.reshape(bs, S5_GROUPS, S5_STATE))
            s5i_p.append(sip.reshape(bp, S5_GROUPS, S5_STATE)); s5i_s.append(sis.reshape(bs, S5_GROUPS, S5_STATE))
        else:
            a = l // 2
            qb, qf, kt_p, vt_p, km = _qkv(xp, g_all, gi, wt_all, qg_all, kg_all, a, bp, kv_p)
            kv_p = (kt_p, vt_p)
            op = _attn_prompt(qb, qf, kt_p, vt_p, a, _group_block_means(km, bp, tp), bias_p)
            xp = _att_out(xp, op, w_o_all, a)
            xs_tb = _to_time_major(xs.reshape(bs, ts, D_MODEL))
            _, qf_s, kt_s, vt_s, _ = _qkv(xs_tb, g_all, gi, wt_all, qg_all, kg_all, a, 1, kv_s)
            kv_s = (kt_s, vt_s)
            q_s = jnp.transpose(qf_s.reshape(ts, bs, D_MODEL), (1, 0, 2))
            kn = jnp.transpose(kt_s[a, 0].reshape(D_MODEL, ts, bs), (2, 1, 0))
            vn = jnp.transpose(vt_s[a, 0].reshape(D_MODEL, ts, bs), (2, 1, 0))
            os_ = _attn_sample(a, page_table, q_s, kn, vn, bias_s, ckt, cvt)
            xs = _att_out(xs, os_.reshape(bs * ts, D_MODEL), w_o_all, a)
        xp = _ffn(xp, g_all, wg_all, wu_all, wd_all, l, 1)
        xs = _ffn(xs, g_all, wg_all, wu_all, wd_all, l, 1)

    n_att = depth // 2
    k_p = jnp.transpose(kv_p[0].reshape(n_att, bp, N_HEADS, HEAD_DIM, tp), (0, 1, 4, 2, 3))
    v_p = jnp.transpose(kv_p[1].reshape(n_att, bp, N_HEADS, HEAD_DIM, tp), (0, 1, 4, 2, 3))
    k_s = jnp.transpose(kv_s[0].reshape(n_att, N_HEADS, HEAD_DIM, ts, bs), (0, 4, 3, 1, 2))
    v_s = jnp.transpose(kv_s[1].reshape(n_att, N_HEADS, HEAD_DIM, ts, bs), (0, 4, 3, 1, 2))
    return (xp.reshape(bp, tp, D_MODEL), xs.reshape(bs, ts, D_MODEL), k_p, v_p, k_s, v_s,
            jnp.stack(lru_p), jnp.stack(lru_s), jnp.stack(conv_p), jnp.stack(conv_s),
            jnp.stack(s5r_p), jnp.stack(s5r_s), jnp.stack(s5i_p), jnp.stack(s5i_s))
```

```python
import functools
import math

import numpy as np
import jax
import jax.numpy as jnp
from jax import lax
from jax.experimental import pallas as pl
from jax.experimental.pallas import tpu as pltpu

f32 = jnp.float32
bf16 = jnp.bfloat16

D_MODEL = 1024
D_FF = 2816
W_LRU = 512
W_S5 = 512
CONV_W = 4
LRU_C = 8.0
S5_GROUPS = 32
S5_STATE = 64
S5_N = S5_GROUPS * S5_STATE
N_HEADS = 16
HEAD_DIM = 64
MOBA_BLOCK = 256
MOBA_TOPK = 3
REL_BUCKETS = 32
REL_MAX_DIST = 128
PAGE_SIZE = 128
EPS = 1e-6

TM = 512
FF_CHUNK = 256
REC_TIME_CHUNK = 64
ATT_HEAD_GROUP = 4
ATT_Q_BLOCKS = 2
ATT_ROW_CHUNK = 64
NEG = -1e30
VMEM_LIMIT = 56 * 1024 * 1024

_DN_T = (((1,), (1,)), ((), ()))


def _cparams(*sem):
    return pltpu.CompilerParams(dimension_semantics=sem, vmem_limit_bytes=VMEM_LIMIT)


def _layer_spec(shape, idx):
    return pl.BlockSpec((None,) + shape, lambda *_: (idx,) + (0,) * len(shape))


def _rms(x, g):
    ms = jnp.mean(x * x, axis=-1, keepdims=True)
    return x * lax.rsqrt(ms + EPS) * g


def _ffn_body(x_ref, g_ref, wg_ref, wu_ref, wd_ref, o_ref):
    x = x_ref[...]
    xn = _rms(x, g_ref[...]).astype(bf16)
    d = None
    for c in range(D_FF // FF_CHUNK):
        cs = slice(c * FF_CHUNK, (c + 1) * FF_CHUNK)
        gate = jnp.dot(xn, wg_ref[:, cs], preferred_element_type=f32)
        up = jnp.dot(xn, wu_ref[:, cs], preferred_element_type=f32)
        h = (gate * jax.nn.sigmoid(gate) * up).astype(bf16)
        dc = jnp.dot(h, wd_ref[cs, :], preferred_element_type=f32)
        d = dc if d is None else d + dc
    o_ref[...] = x + 0.5 * d


def _ffn(x, g, wg, wu, wd, l, s):
    n = x.shape[0]
    return pl.pallas_call(
        _ffn_body,
        grid=(n // TM,),
        in_specs=[
            pl.BlockSpec((TM, D_MODEL), lambda i: (i, 0)),
            _layer_spec((1, D_MODEL), 3 * l + 2 * s),
            _layer_spec((D_MODEL, D_FF), 2 * l + s),
            _layer_spec((D_MODEL, D_FF), 2 * l + s),
            _layer_spec((D_FF, D_MODEL), 2 * l + s),
        ],
        out_specs=pl.BlockSpec((TM, D_MODEL), lambda i: (i, 0)),
        out_shape=jax.ShapeDtypeStruct((n, D_MODEL), f32),
        compiler_params=_cparams("parallel"),
        name="ffn",
    )(x, g, wg, wu, wd)


def _att_out_body(x_ref, y_ref, w_ref, o_ref):
    o_ref[...] = x_ref[...] + jnp.dot(y_ref[...].astype(bf16), w_ref[...], preferred_element_type=f32)


def _att_out(x, y, w, a):
    n = x.shape[0]
    return pl.pallas_call(
        _att_out_body,
        grid=(n // TM,),
        in_specs=[
            pl.BlockSpec((TM, D_MODEL), lambda i: (i, 0)),
            pl.BlockSpec((TM, D_MODEL), lambda i: (i, 0)),
            _layer_spec((D_MODEL, D_MODEL), a),
        ],
        out_specs=pl.BlockSpec((TM, D_MODEL), lambda i: (i, 0)),
        out_shape=jax.ShapeDtypeStruct((n, D_MODEL), f32),
        compiler_params=_cparams("parallel"),
        name="att_out",
    )(x, y, w)


def _rec_out_body(x_ref, y_ref, gw_ref, gb_ref, w_ref, o_ref):
    y = y_ref[...]
    ya = y[:, :W_LRU].astype(bf16)
    g5 = y[:, W_LRU:]
    glu = jax.nn.sigmoid(jnp.dot(g5.astype(bf16), gw_ref[...], preferred_element_type=f32) + gb_ref[...])
    yb = (g5 * glu).astype(bf16)
    yy = jnp.concatenate([ya, yb], axis=-1)
    o_ref[...] = x_ref[...] + jnp.dot(yy, w_ref[...], preferred_element_type=f32)


def _rec_out(x, y, glu_w, glu_b, w_out, r):
    n = x.shape[0]
    return pl.pallas_call(
        _rec_out_body,
        grid=(n // TM,),
        in_specs=[
            pl.BlockSpec((TM, D_MODEL), lambda i: (i, 0)),
            pl.BlockSpec((TM, D_MODEL), lambda i: (i, 0)),
            _layer_spec((W_S5, W_S5), r),
            _layer_spec((1, W_S5), r),
            _layer_spec((D_MODEL, D_MODEL), r),
        ],
        out_specs=pl.BlockSpec((TM, D_MODEL), lambda i: (i, 0)),
        out_shape=jax.ShapeDtypeStruct((n, D_MODEL), f32),
        compiler_params=_cparams("parallel"),
        name="rec_out",
    )(x, y, glu_w, glu_b, w_out)


def _rec_scan_body(bsz, tc, x_ref, g_ref, win_ref, conv0_ref, lru0_ref, s5r0_ref, s5i0_ref, cw_ref, cb_ref, wr_ref,
                   br_ref, wi_ref, bi_ref, lam_ref, ar_ref, ai_ref, bbr_ref, bbi_ref, ccr_ref, cci_ref, d_ref,
                   y_ref, conv_out, lru_out, s5r_out, s5i_out,
                   z_ref, xbuf, a_scr, u_scr, hr_scr, hi_scr, lru_c, s5r_c, s5i_c):
    rows = bsz * tc
    step = pl.program_id(0)
    last = pl.num_programs(0) - 1
    z_ref[...] = jnp.dot(_rms(x_ref[...], g_ref[...]).astype(bf16), win_ref[...], preferred_element_type=f32)

    @pl.when(step == 0)
    def _():
        xbuf[0:3 * bsz, :] = conv0_ref[...]
        lru_c[...] = lru0_ref[...]
        s5r_c[...] = s5r0_ref[...]
        s5i_c[...] = s5i0_ref[...]

    xbuf[3 * bsz:, :] = z_ref[:, 0:W_LRU]
    xc = cb_ref[...] + cw_ref[0:1, :] * xbuf[0:rows, :]
    for j in range(1, CONV_W):
        xc = xc + cw_ref[j:j + 1, :] * xbuf[j * bsz:j * bsz + rows, :]
    tail = xbuf[rows:rows + 3 * bsz, :]

    xcb = xc.astype(bf16)
    r = jax.nn.sigmoid(jnp.dot(xcb, wr_ref[...], preferred_element_type=f32) + br_ref[...])
    ig = jax.nn.sigmoid(jnp.dot(xcb, wi_ref[...], preferred_element_type=f32) + bi_ref[...])
    lam = lam_ref[...]
    log_sig = jnp.minimum(lam, 0.0) - jnp.log1p(jnp.exp(-jnp.abs(lam)))
    log_a = (LRU_C * r) * log_sig
    a = jnp.exp(log_a)
    mult = jnp.sqrt(-jnp.tanh(log_a) * (a * a + 1.0))
    a_scr[...] = a
    u_scr[...] = mult * (ig * xc)

    def lru_step(t, h):
        rs = pl.ds(pl.multiple_of(t * bsz, 8), bsz)
        h = a_scr[rs, :] * h + u_scr[rs, :]
        u_scr[rs, :] = h
        return h

    h_last = lax.fori_loop(0, tc, lru_step, lru_c[...], unroll=True)
    lru_c[...] = h_last
    y_ref[:, 0:W_LRU] = u_scr[...] * jax.nn.gelu(z_ref[:, W_LRU:2 * W_LRU])

    half_in, half_st = W_S5 // 2, S5_N // 2
    for k in range(2):
        ubk = z_ref[:, 2 * W_LRU + k * half_in:2 * W_LRU + (k + 1) * half_in].astype(bf16)
        hr_scr[:, k * half_st:(k + 1) * half_st] = jnp.dot(ubk, bbr_ref[k], preferred_element_type=f32)
        hi_scr[:, k * half_st:(k + 1) * half_st] = jnp.dot(ubk, bbi_ref[k], preferred_element_type=f32)

    lane_chunk = 512
    for c in range(S5_N // lane_chunk):
        cs = slice(c * lane_chunk, (c + 1) * lane_chunk)
        ar = jnp.broadcast_to(ar_ref[:, cs], (bsz, lane_chunk))
        ai = jnp.broadcast_to(ai_ref[:, cs], (bsz, lane_chunk))

        def s5_step(t, carry, cs=cs, ar=ar, ai=ai):
            hr, hi = carry
            rs = pl.ds(pl.multiple_of(t * bsz, 8), bsz)
            nr = ar * hr - ai * hi + hr_scr[rs, cs]
            ni = ar * hi + ai * hr + hi_scr[rs, cs]
            hr_scr[rs, cs] = nr
            hi_scr[rs, cs] = ni
            return nr, ni

        hr, hi = lax.fori_loop(0, tc, s5_step, (s5r_c[:, cs], s5i_c[:, cs]), unroll=True)
        s5r_c[:, cs] = hr
        s5i_c[:, cs] = hi

    for k in range(2):
        hrk = hr_scr[:, k * half_st:(k + 1) * half_st].astype(bf16)
        hik = hi_scr[:, k * half_st:(k + 1) * half_st].astype(bf16)
        ys = (jnp.dot(hrk, ccr_ref[k], preferred_element_type=f32)
              - jnp.dot(hik, cci_ref[k], preferred_element_type=f32))
        ub = z_ref[:, 2 * W_LRU + k * half_in:2 * W_LRU + (k + 1) * half_in]
        ys = ys + d_ref[:, k * half_in:(k + 1) * half_in] * ub
        y_ref[:, W_LRU + k * half_in:W_LRU + (k + 1) * half_in] = jax.nn.gelu(ys)

    xbuf[0:3 * bsz, :] = tail

    @pl.when(step == last)
    def _():
        conv_out[...] = tail
        lru_out[...] = lru_c[...]
        s5r_out[...] = s5r_c[...]
        s5i_out[...] = s5i_c[...]


def _rec_scan(x, bsz, tc, g, gi, w_in, conv0, lru0, s5r0, s5i0, prm, r):
    n = x.shape[0]
    rows = bsz * tc
    full = lambda shape: pl.BlockSpec(shape, lambda i: (0,) * len(shape))
    in_specs = [
        pl.BlockSpec((rows, D_MODEL), lambda i: (i, 0)),
        _layer_spec((1, D_MODEL), gi),
        _layer_spec((D_MODEL, 3 * W_LRU), r),
        full((3 * bsz, W_LRU)), full((bsz, W_LRU)), full((bsz, S5_N)), full((bsz, S5_N)),
    ] + [_layer_spec(p.shape[1:], r) for p in prm]
    out_shape = (
        jax.ShapeDtypeStruct((n, D_MODEL), f32),
        jax.ShapeDtypeStruct((3 * bsz, W_LRU), f32),
        jax.ShapeDtypeStruct((bsz, W_LRU), f32),
        jax.ShapeDtypeStruct((bsz, S5_N), f32),
        jax.ShapeDtypeStruct((bsz, S5_N), f32),
    )
    out_specs = (
        pl.BlockSpec((rows, D_MODEL), lambda i: (i, 0)),
        full((3 * bsz, W_LRU)), full((bsz, W_LRU)), full((bsz, S5_N)), full((bsz, S5_N)),
    )
    scratch = [
        pltpu.VMEM((rows, 3 * W_LRU), f32),
        pltpu.VMEM((rows + 3 * bsz, W_LRU), f32),
        pltpu.VMEM((rows, W_LRU), f32), pltpu.VMEM((rows, W_LRU), f32),
        pltpu.VMEM((rows, S5_N), f32), pltpu.VMEM((rows, S5_N), f32),
        pltpu.VMEM((bsz, W_LRU), f32), pltpu.VMEM((bsz, S5_N), f32), pltpu.VMEM((bsz, S5_N), f32),
    ]
    return pl.pallas_call(
        functools.partial(_rec_scan_body, bsz, tc),
        grid=(n // rows,),
        in_specs=in_specs,
        out_specs=out_specs,
        out_shape=out_shape,
        scratch_shapes=scratch,
        compiler_params=_cparams("arbitrary"),
        name="rec_scan",
    )(x, g, w_in, conv0, lru0, s5r0, s5i0, *prm)


def _head_norm_t(t, gain):
    n = t.shape[1]
    t3 = t.reshape(N_HEADS, HEAD_DIM, n)
    ms = jnp.mean(t3 * t3, axis=1, keepdims=True)
    return (t3 * lax.rsqrt(ms + EPS) * gain[None]).reshape(N_HEADS * HEAD_DIM, n)


def _qkv_body(n_alias, x_ref, g_ref, wt_ref, qg_ref, kg_ref, *refs):
    qb_ref, qf_ref, kt_ref, vt_ref, km_ref = refs[n_alias:]
    xn = _rms(x_ref[...], g_ref[...])
    xnt = xn.T.astype(bf16)
    qkvt = jnp.dot(wt_ref[...], xnt, preferred_element_type=f32)
    qt = _head_norm_t(qkvt[0:D_MODEL], qg_ref[...])
    kt = _head_norm_t(qkvt[D_MODEL:2 * D_MODEL], kg_ref[...])
    q = qt.T
    qf_ref[...] = q
    qb_ref[...] = (q * (HEAD_DIM ** -0.5)).astype(bf16)
    if n_alias:
        kt_ref[0] = kt
        vt_ref[0] = qkvt[2 * D_MODEL:]
    else:
        for j in range(kt_ref.shape[0]):
            kt_ref[j, 0] = kt
            vt_ref[j, 0] = qkvt[2 * D_MODEL:]
    for n in range(TM // MOBA_BLOCK):
        km_ref[0, :, n:n + 1] = jnp.mean(kt[:, n * MOBA_BLOCK:(n + 1) * MOBA_BLOCK], axis=1, keepdims=True)


def _qkv(x, g, gi, wt, qg, kg, a, nb, kv_prev=None):
    n = x.shape[0]
    tlen = n // nb
    tpb = tlen // TM
    n_att = wt.shape[0]
    prev = () if kv_prev is None else tuple(kv_prev)
    if prev:
        kv_spec = pl.BlockSpec((None, 1, D_MODEL, TM), lambda i: (a, i // tpb, 0, i % tpb))
    else:
        kv_spec = pl.BlockSpec((n_att, 1, D_MODEL, TM), lambda i: (0, i // tpb, 0, i % tpb))
    return pl.pallas_call(
        functools.partial(_qkv_body, len(prev)),
        grid=(n // TM,),
        in_specs=[
            pl.BlockSpec((TM, D_MODEL), lambda i: (i, 0)),
            _layer_spec((1, D_MODEL), gi),
            _layer_spec((3 * D_MODEL, D_MODEL), a),
            _layer_spec((HEAD_DIM, TM), a),
            _layer_spec((HEAD_DIM, TM), a),
        ] + [pl.BlockSpec(memory_space=pl.ANY)] * len(prev),
        out_specs=(
            pl.BlockSpec((TM, D_MODEL), lambda i: (i, 0)),
            pl.BlockSpec((TM, D_MODEL), lambda i: (i, 0)),
            kv_spec,
            kv_spec,
            pl.BlockSpec((1, D_MODEL, TM // MOBA_BLOCK), lambda i: (i, 0, 0)),
        ),
        out_shape=(
            jax.ShapeDtypeStruct((n, D_MODEL), bf16),
            jax.ShapeDtypeStruct((n, D_MODEL), f32),
            jax.ShapeDtypeStruct((n_att, nb, D_MODEL, tlen), f32),
            jax.ShapeDtypeStruct((n_att, nb, D_MODEL, tlen), f32),
            jax.ShapeDtypeStruct((n // TM, D_MODEL, TM // MOBA_BLOCK), f32),
        ),
        input_output_aliases={5 + j: 2 + j for j in range(len(prev))},
        compiler_params=_cparams("parallel"),
        name="qkv",
    )(x, g, wt, qg, kg, *prev)


def _split_bf16(x):
    hi = x.astype(bf16)
    lo = (x - hi.astype(f32)).astype(bf16)
    return hi, lo


def _attn_prompt_body(nblk, hg, nqb, qb_ref, qf_ref, kt_ref, vt_ref, km_ref, bias_ref, o_ref,
                      ka_scr, va_scr, qa_scr, s_scr, p_scr):
    step = pl.program_id(2)
    pw = 2 * HEAD_DIM
    tlen = kt_ref.shape[2]
    qrows = nqb * MOBA_BLOCK
    lane = lax.broadcasted_iota(jnp.int32, (1, pw), 1)
    nio = lax.broadcasted_iota(jnp.int32, (nblk, MOBA_BLOCK), 0)

    @pl.when(step == 0)
    def _():
        blk_rows = lax.broadcasted_iota(jnp.int32, (pw, tlen), 0)
        blk_cols = lax.broadcasted_iota(jnp.int32, (pw, tlen), 1) // MOBA_BLOCK
        onehot = jnp.where((blk_rows == blk_cols) | (blk_rows == blk_cols + nblk), 1.0, 0.0).astype(bf16)
        for pp in range(hg // 2):
            ka_scr[pp, 0:pw, :] = kt_ref[0, pp * pw:(pp + 1) * pw, :].astype(bf16)
            ka_scr[pp, pw:2 * pw, :] = onehot
            va_scr[pp, 0:pw, :] = vt_ref[0, pp * pw:(pp + 1) * pw, :].astype(bf16)
            va_scr[pp, pw:2 * pw, :] = jnp.ones((pw, tlen), bf16)

    gates = {}
    for pp in range(hg // 2):
        ps = slice(pp * pw, (pp + 1) * pw)
        km = km_ref[0, 0, :, ps]
        k_hi, k_lo = _split_bf16(jnp.concatenate([jnp.where(lane < HEAD_DIM, km, 0.0),
                                                  jnp.where(lane < HEAD_DIM, 0.0, km)], axis=0))
        for qi in range(nqb):
            q_hi, q_lo = _split_bf16(qf_ref[qi * MOBA_BLOCK:(qi + 1) * MOBA_BLOCK, ps])
            gates[pp, qi] = (lax.dot_general(k_hi, q_hi, _DN_T, preferred_element_type=f32)
                             + lax.dot_general(k_lo, q_hi, _DN_T, preferred_element_type=f32)
                             + lax.dot_general(k_hi, q_lo, _DN_T, preferred_element_type=f32))

    for h in range(hg):
        ps = slice((h // 2) * pw, (h // 2 + 1) * pw)
        hmask = (lane // HEAD_DIM) == h % 2
        b_far = bias_ref[0, h, 2, 0:1, 0:1]
        b_hi = b_far.astype(bf16).astype(f32)
        for qi in range(nqb):
            i = step * nqb + qi
            rows = slice(qi * MOBA_BLOCK, (qi + 1) * MOBA_BLOCK)
            qa_scr[h, rows, 0:pw] = jnp.where(hmask, qb_ref[rows, ps], jnp.zeros((), bf16))
            gate = gates[h // 2, qi][(h % 2) * 16:(h % 2) * 16 + nblk]
            rank = jnp.zeros((nblk, MOBA_BLOCK), f32)
            for m in range(nblk):
                gm = gate[m:m + 1, :]
                beats = jnp.where(gm > gate, 1.0, jnp.where(gm == gate, jnp.where(m < nio, 1.0, 0.0), 0.0))
                rank = rank + jnp.where(m < i, beats, 0.0)
            keep = jnp.where(nio < i, jnp.where(rank < MOBA_TOPK, 1.0, 0.0), jnp.where(nio == i, 1.0, 0.0)) > 0.5
            far = nio < i - 1
            term_hi = jnp.where(keep, jnp.where(far, b_hi, 0.0), NEG)
            term_lo = jnp.where(keep & far, b_far - b_hi, 0.0)
            terms = jnp.concatenate([term_hi, term_lo, jnp.zeros((pw - 2 * nblk, MOBA_BLOCK), f32)], axis=0).T
            qa_scr[h, rows, pw:2 * pw] = terms.astype(bf16)

    nrow = 2 * qrows

    def process(width):
        def scores(h):
            hrows = slice((h % 2) * qrows, (h % 2 + 1) * qrows)
            s_scr[h // 2, hrows, 0:width] = jnp.dot(qa_scr[h], ka_scr[h // 2, :, 0:width], preferred_element_type=f32)
            for qi in range(nqb):
                i = step * nqb + qi
                rows = slice((h % 2) * qrows + qi * MOBA_BLOCK, (h % 2) * qrows + (qi + 1) * MOBA_BLOCK)
                near = pl.multiple_of(i * MOBA_BLOCK, MOBA_BLOCK)
                prev = pl.multiple_of(jnp.maximum(i - 1, 0) * MOBA_BLOCK, MOBA_BLOCK)
                prev_tile = jnp.where(i >= 1, 1, 3)
                s_scr[h // 2, rows, pl.ds(near, MOBA_BLOCK)] += bias_ref[0, h, 0]
                s_scr[h // 2, rows, pl.ds(prev, MOBA_BLOCK)] += bias_ref[0, h, prev_tile]
        def softmax_pv(h):
            pp, base = h // 2, (h % 2) * qrows
            for r in range(qrows // ATT_ROW_CHUNK):
                rs = slice(base + r * ATT_ROW_CHUNK, base + (r + 1) * ATT_ROW_CHUNK)
                mx = s_scr[pp, rs, 0:MOBA_BLOCK]
                for t in range(1, width // MOBA_BLOCK):
                    mx = jnp.maximum(mx, s_scr[pp, rs, t * MOBA_BLOCK:(t + 1) * MOBA_BLOCK])
                mx = jnp.max(mx, axis=-1, keepdims=True)
                for t in range(width // MOBA_BLOCK):
                    cs = slice(t * MOBA_BLOCK, (t + 1) * MOBA_BLOCK)
                    p_scr[pp, rs, cs] = jnp.exp(s_scr[pp, rs, cs] - mx).astype(bf16)
            o = lax.dot_general(p_scr[pp, base:base + qrows, 0:width], va_scr[pp, :, 0:width], _DN_T,
                                preferred_element_type=f32)
            return o[:, 0:pw] / o[:, pw:2 * pw]

        for h in range(hg):
            scores(h)
        res = [softmax_pv(h) for h in range(hg)]
        for pp in range(hg // 2):
            o_ref[:, pp * pw:(pp + 1) * pw] = jnp.where(lane < HEAD_DIM, res[2 * pp], res[2 * pp + 1]).astype(o_ref.dtype)

    for k in range(nblk // nqb):
        @pl.when(step == k)
        def _(k=k):
            process((k + 1) * qrows)


def _attn_prompt(qb, qf, kt, vt, a, km, bias):
    _, nb, _, tlen = kt.shape
    nq = tlen // MOBA_BLOCK
    hg, nqb = ATT_HEAD_GROUP, ATT_Q_BLOCKS
    gw = hg * HEAD_DIM
    qrows = nqb * MOBA_BLOCK
    steps = nq // nqb
    kv_spec = pl.BlockSpec((None, 1, gw, tlen), lambda p, b, i: (a, b, p, 0))
    return pl.pallas_call(
        functools.partial(_attn_prompt_body, nq, hg, nqb),
        grid=(N_HEADS // hg, nb, steps),
        in_specs=[
            pl.BlockSpec((qrows, gw), lambda p, b, i: (b * steps + i, p)),
            pl.BlockSpec((qrows, gw), lambda p, b, i: (b * steps + i, p)),
            kv_spec,
            kv_spec,
            pl.BlockSpec((1, 1, km.shape[2], gw), lambda p, b, i: (b, p, 0, 0)),
            pl.BlockSpec((1, hg, 4, MOBA_BLOCK, MOBA_BLOCK), lambda p, b, i: (p, 0, 0, 0, 0)),
        ],
        out_specs=pl.BlockSpec((qrows, gw), lambda p, b, i: (b * steps + i, p)),
        out_shape=jax.ShapeDtypeStruct((nb * tlen, D_MODEL), bf16),
        scratch_shapes=[
            pltpu.VMEM((hg // 2, 4 * HEAD_DIM, tlen), bf16),
            pltpu.VMEM((hg // 2, 4 * HEAD_DIM, tlen), bf16),
            pltpu.VMEM((hg, qrows, 4 * HEAD_DIM), bf16),
            pltpu.VMEM((hg // 2, 2 * qrows, tlen), f32),
            pltpu.VMEM((hg // 2, 2 * qrows, tlen), bf16),
        ],
        compiler_params=_cparams("parallel", "parallel", "arbitrary"),
        name="attn_prompt",
    )(qb, qf, kt, vt, km, bias)


def _attn_sample_body(layer, n_pages, pt_ref, q_ref, kn_ref, vn_ref, bias_ref, ck_hbm, cv_hbm, o_ref,
                      kbuf, vbuf, sems, s_scr, p_scr):
    b = pl.program_id(0)
    slot = b % 2
    nq = q_ref.shape[1]
    npg = n_pages + 1
    rows = N_HEADS * 8

    def k_copy(bb, sl, p):
        return pltpu.make_async_copy(ck_hbm.at[layer, pt_ref[bb, p]], kbuf.at[sl, p], sems.at[0, sl, p])

    def v_copy(bb, sl, p):
        return pltpu.make_async_copy(cv_hbm.at[layer, pt_ref[bb, p]], vbuf.at[sl, p], sems.at[1, sl, p])

    def start_all(bb, sl):
        for p in range(n_pages):
            k_copy(bb, sl, p).start()
        for p in range(n_pages):
            v_copy(bb, sl, p).start()

    @pl.when(b == 0)
    def _():
        start_all(0, 0)

    @pl.when(b + 1 < pl.num_programs(0))
    def _():
        start_all(b + 1, 1 - slot)

    q8 = jnp.concatenate([q_ref[0] * (HEAD_DIM ** -0.5), jnp.zeros((8 - nq, D_MODEL), f32)], axis=0)
    rh = lax.broadcasted_iota(jnp.int32, (rows, D_MODEL), 0) // 8
    ch = lax.broadcasted_iota(jnp.int32, (rows, D_MODEL), 1) // HEAD_DIM
    qblk = jnp.where(rh == ch, jnp.tile(q8, (N_HEADS, 1)), 0.0).astype(bf16)

    def page_scores(kpage):
        kp = kpage.reshape(N_HEADS * HEAD_DIM, PAGE_SIZE).astype(bf16)
        return jnp.dot(qblk, kp, preferred_element_type=f32).reshape(N_HEADS, 8, PAGE_SIZE)

    for p in range(n_pages):
        k_copy(b, slot, p).wait()
    for p in range(n_pages):
        s_scr[p] = page_scores(kbuf[slot, p])

    def new_page(ref):
        return jnp.concatenate([ref[0], jnp.zeros((PAGE_SIZE - nq, D_MODEL), f32)], axis=0).astype(bf16)

    s_scr[n_pages] = lax.dot_general(qblk, new_page(kn_ref), _DN_T,
                                     preferred_element_type=f32).reshape(N_HEADS, 8, PAGE_SIZE)

    ppb = MOBA_BLOCK // PAGE_SIZE
    nblk = n_pages // ppb
    gates = []
    for n in range(nblk):
        blk = s_scr[n * ppb]
        for e in range(1, ppb):
            blk = blk + s_scr[n * ppb + e]
        gates.append(jnp.sum(blk, axis=-1, keepdims=True))
    sel = []
    for n in range(nblk):
        rank = jnp.zeros_like(gates[n])
        for m in range(nblk):
            if m == n:
                continue
            if m < n:
                rank = rank + jnp.where(gates[m] >= gates[n], 1.0, 0.0)
            else:
                rank = rank + jnp.where(gates[m] > gates[n], 1.0, 0.0)
        sel.append(rank < MOBA_TOPK)

    mx_e = jnp.full((N_HEADS, 8, PAGE_SIZE), -1e38, f32)
    for p in range(npg):
        s = s_scr[p] + bias_ref[p]
        if p < n_pages:
            s = jnp.where(sel[p // ppb], s, NEG)
        s_scr[p] = s
        mx_e = jnp.maximum(mx_e, s)
    mx = jnp.max(mx_e, axis=-1, keepdims=True)
    den_e = jnp.zeros((N_HEADS, 8, PAGE_SIZE), f32)
    for p in range(npg):
        e = jnp.exp(s_scr[p] - mx)
        p_scr[p] = e
        den_e = den_e + e
    inv = 1.0 / jnp.sum(den_e, axis=-1, keepdims=True)

    for p in range(n_pages):
        v_copy(b, slot, p).wait()

    def page_pv(p, vpage):
        pr = (p_scr[p] * inv).reshape(rows, PAGE_SIZE).astype(bf16)
        vp = vpage.reshape(N_HEADS * HEAD_DIM, PAGE_SIZE).astype(bf16)
        return lax.dot_general(pr, vp, _DN_T, preferred_element_type=f32)

    pr_new = (p_scr[n_pages] * inv).reshape(rows, PAGE_SIZE).astype(bf16)
    acc = jnp.dot(pr_new, new_page(vn_ref), preferred_element_type=f32)
    for p in range(n_pages):
        acc = acc + page_pv(p, vbuf[slot, p])
    lane_head = lax.broadcasted_iota(jnp.int32, (8, D_MODEL), 1) // HEAD_DIM
    out = jnp.zeros((8, D_MODEL), f32)
    for h in range(N_HEADS):
        out = out + jnp.where(lane_head == h, acc[h * 8:(h + 1) * 8, :], 0.0)
    o_ref[0] = out[0:nq, :].astype(o_ref.dtype)


def _attn_sample(layer, page_table, q, kn, vn, bias, ck, cv):
    nb, nq, _ = q.shape
    n_pages = page_table.shape[1]
    npg = n_pages + 1
    grid_spec = pltpu.PrefetchScalarGridSpec(
        num_scalar_prefetch=1,
        grid=(nb,),
        in_specs=[
            pl.BlockSpec((1, nq, D_MODEL), lambda b, pt: (b, 0, 0)),
            pl.BlockSpec((1, nq, D_MODEL), lambda b, pt: (b, 0, 0)),
            pl.BlockSpec((1, nq, D_MODEL), lambda b, pt: (b, 0, 0)),
            pl.BlockSpec((npg, N_HEADS, 8, PAGE_SIZE), lambda b, pt: (0, 0, 0, 0)),
            pl.BlockSpec(memory_space=pl.ANY),
            pl.BlockSpec(memory_space=pl.ANY),
        ],
        out_specs=pl.BlockSpec((1, nq, D_MODEL), lambda b, pt: (b, 0, 0)),
        scratch_shapes=[
            pltpu.VMEM((2, n_pages, N_HEADS, HEAD_DIM, PAGE_SIZE), f32),
            pltpu.VMEM((2, n_pages, N_HEADS, HEAD_DIM, PAGE_SIZE), f32),
            pltpu.SemaphoreType.DMA((2, 2, n_pages)),
            pltpu.VMEM((npg, N_HEADS, 8, PAGE_SIZE), f32),
            pltpu.VMEM((npg, N_HEADS, 8, PAGE_SIZE), f32),
        ],
    )
    return pl.pallas_call(
        functools.partial(_attn_sample_body, layer, n_pages),
        grid_spec=grid_spec,
        out_shape=jax.ShapeDtypeStruct((nb, nq, D_MODEL), f32),
        compiler_params=_cparams("arbitrary"),
        name="attn_sample",
    )(page_table, q, kn, vn, bias, ck, cv)


def _t5_bucket_np(dist):
    dist = np.asarray(dist)
    max_exact = REL_BUCKETS // 2
    log_ratio = (np.log(np.maximum(dist, 1).astype(np.float32) / np.float32(max_exact))
                 / np.float32(math.log(REL_MAX_DIST / max_exact))).astype(np.float32)
    large = np.minimum(max_exact + (log_ratio * np.float32(REL_BUCKETS - max_exact)).astype(np.int32), REL_BUCKETS - 1)
    return np.where(dist < max_exact, dist, large)


def _prompt_bias(rel_bias):
    blk, span = MOBA_BLOCK, 2 * MOBA_BLOCK
    by_dist = rel_bias[_t5_bucket_np(np.arange(3 * blk))].T
    j = np.arange(span)
    j = np.where(j < blk, j, j - span)
    tiles = []
    for d in range(3):
        dist = d * blk - j
        w = jnp.where((dist >= 0)[None, :], by_dist[:, np.clip(dist, 0, 3 * blk - 1)], NEG)
        t = jnp.tile(w, (1, blk))[:, :blk * (span - 1)].reshape(N_HEADS, blk, span - 1)
        tiles.append(t[:, :, :blk])
    tiles.append(jnp.zeros_like(tiles[0]))
    t = jnp.stack(tiles, axis=1)
    return t.reshape(N_HEADS // ATT_HEAD_GROUP, ATT_HEAD_GROUP, 4, blk, blk)


def _group_block_means(km, nb, tlen):
    gw = ATT_HEAD_GROUP * HEAD_DIM
    nblk = tlen // MOBA_BLOCK
    km = jnp.transpose(km.reshape(nb, tlen // TM, N_HEADS // ATT_HEAD_GROUP, gw, TM // MOBA_BLOCK), (0, 2, 1, 4, 3))
    km = km.reshape(nb, N_HEADS // ATT_HEAD_GROUP, nblk, gw)
    return jnp.pad(km, ((0, 0), (0, 0), (0, 16 - nblk), (0, 0)))


def _sample_bias(rel_bias, n_pages, nq):
    past = n_pages * PAGE_SIZE
    nkeys = (n_pages + 1) * PAGE_SIZE
    span = past + 8
    by_dist = rel_bias[_t5_bucket_np(np.arange(span))].T
    rev = jnp.pad(by_dist[:, ::-1], ((0, 0), (0, nkeys)))
    rows = jnp.stack([rev[:, 7 - t:7 - t + nkeys] for t in range(8)], axis=1)
    kpos = np.arange(nkeys)
    t8 = np.arange(8)[:, None]
    valid = (past + t8 - kpos[None, :] >= 0) & (t8 < nq) & (kpos[None, :] < past + nq)
    fill = np.where(t8 < nq, NEG, 0.0).astype(np.float32)
    b = jnp.where(valid[None], rows, fill[None])
    return jnp.transpose(b.reshape(N_HEADS, 8, n_pages + 1, PAGE_SIZE), (2, 0, 1, 3))


def _block_diag(w):
    g, a, b = w.shape
    eye = jnp.eye(g, dtype=w.dtype)
    return jnp.einsum('gab,gh->gahb', w, eye).reshape(g * a, g * b)


def _rec_params(conv_w, conv_b, w_r, b_r, w_i, b_i, lam, lam_re, lam_im, log_step, b_re, b_im, c_re, c_im, d):
    dt = jnp.exp(log_step)[:, None]
    mag = jnp.exp(lam_re * dt)
    ar, ai = mag * jnp.cos(lam_im * dt), mag * jnp.sin(lam_im * dt)
    den = lam_re * lam_re + lam_im * lam_im
    qr = ((ar - 1.0) * lam_re + ai * lam_im) / den
    qi = (ai * lam_re - (ar - 1.0) * lam_im) / den
    bbr = qr[..., None] * b_re - qi[..., None] * b_im
    bbi = qr[..., None] * b_im + qi[..., None] * b_re

    def halves(m):
        return jnp.stack([m[:m.shape[0] // 2, :m.shape[1] // 2], m[m.shape[0] // 2:, m.shape[1] // 2:]]).astype(bf16)

    bbr_m = halves(_block_diag(jnp.transpose(bbr, (0, 2, 1))))
    bbi_m = halves(_block_diag(jnp.transpose(bbi, (0, 2, 1))))
    ccr_m = halves(_block_diag(jnp.transpose(c_re, (0, 2, 1))))
    cci_m = halves(_block_diag(jnp.transpose(c_im, (0, 2, 1))))
    return (conv_w, conv_b[None], _block_diag(w_r).astype(bf16), b_r[None], _block_diag(w_i).astype(bf16), b_i[None],
            lam[None], ar.reshape(1, S5_N), ai.reshape(1, S5_N), bbr_m, bbi_m, ccr_m, cci_m, d[None])


def _to_time_major(x):
    b, t, c = x.shape
    return jnp.transpose(x, (1, 0, 2)).reshape(t * b, c)


def _from_time_major(y, b):
    n, c = y.shape
    return jnp.transpose(y.reshape(n // b, b, c), (1, 0, 2)).reshape(n, c)


def _recurrent_layer(x2d, bsz, tc, g, gi, w_in, conv0, lru0, s5r0, s5i0, prm, glu_w, glu_b, w_out, r):
    n = x2d.shape[0]
    x_tb = _to_time_major(x2d.reshape(bsz, n // bsz, D_MODEL))
    y_tb, conv_n, lru_n, s5r_n, s5i_n = _rec_scan(x_tb, bsz, tc, g, gi, w_in, conv0, lru0, s5r0, s5i0, prm, r)
    y = _from_time_major(y_tb, bsz)
    return _rec_out(x2d, y, glu_w, glu_b, w_out, r), conv_n, lru_n, s5r_n, s5i_n


def kernel(x_prompt, x_sample, cache_k, cache_v, state_lru_h, state_conv, state_s5_re, state_s5_im, page_table, norm_g, ffn_w_gate, ffn_w_up, ffn_w_down, rec_w_in, rec_conv_w, rec_conv_b, lru_w_r, lru_b_r, lru_w_i, lru_b_i, lru_lambda, s5_lambda_re, s5_lambda_im, s5_log_step, s5_b_re, s5_b_im, s5_c_re, s5_c_im, s5_d, s5_glu_w, s5_glu_b, rec_w_out, att_w_qkv, att_q_norm, att_k_norm, att_w_o, rel_bias):
    bp, tp, _ = x_prompt.shape
    bs, ts, _ = x_sample.shape
    depth = norm_g.shape[0]
    n_pages = page_table.shape[1]
    xp = x_prompt.reshape(bp * tp, D_MODEL)
    xs = x_sample.reshape(bs * ts, D_MODEL)

    g_all = norm_g.reshape(depth * 3, 1, D_MODEL)
    wg_all = ffn_w_gate.astype(bf16).reshape(depth * 2, D_MODEL, D_FF)
    wu_all = ffn_w_up.astype(bf16).reshape(depth * 2, D_MODEL, D_FF)
    wd_all = ffn_w_down.astype(bf16).reshape(depth * 2, D_FF, D_MODEL)
    w_in_all = rec_w_in.astype(bf16)
    glu_w_all = s5_glu_w.astype(bf16)
    glu_b_all = s5_glu_b[:, None, :]
    w_out_all = rec_w_out.astype(bf16)
    rec_prm = jax.vmap(_rec_params)(rec_conv_w, rec_conv_b, lru_w_r, lru_b_r, lru_w_i, lru_b_i, lru_lambda,
                                    s5_lambda_re, s5_lambda_im, s5_log_step, s5_b_re, s5_b_im, s5_c_re, s5_c_im, s5_d)
    wt_all = jnp.transpose(att_w_qkv, (0, 2, 1)).astype(bf16)
    qg_all = jnp.broadcast_to(att_q_norm[:, :, None], att_q_norm.shape + (TM,))
    kg_all = jnp.broadcast_to(att_k_norm[:, :, None], att_k_norm.shape + (TM,))
    w_o_all = att_w_o.astype(bf16)
    ckt = jnp.transpose(cache_k, (0, 1, 3, 4, 2))
    cvt = jnp.transpose(cache_v, (0, 1, 3, 4, 2))
    bias_p = _prompt_bias(rel_bias)
    bias_s = _sample_bias(rel_bias, n_pages, ts)

    kv_p = kv_s = None
    lru_p, lru_s, conv_p, conv_s, s5r_p, s5r_s, s5i_p, s5i_s = [], [], [], [], [], [], [], []
    for l in range(depth):
        xp = _ffn(xp, g_all, wg_all, wu_all, wd_all, l, 0)
        xs = _ffn(xs, g_all, wg_all, wu_all, wd_all, l, 0)
        gi = 3 * l + 1
        if l % 2 == 0:
            r = l // 2
            xp, cp, lp, srp, sip = _recurrent_layer(
                xp, bp, REC_TIME_CHUNK, g_all, gi, w_in_all, jnp.zeros((3 * bp, W_LRU), f32),
                jnp.zeros((bp, W_LRU), f32), jnp.zeros((bp, S5_N), f32), jnp.zeros((bp, S5_N), f32),
                rec_prm, glu_w_all, glu_b_all, w_out_all, r)
            conv0 = jnp.transpose(state_conv[r], (1, 0, 2)).reshape(3 * bs, W_LRU)
            xs, cs, ls, srs, sis = _recurrent_layer(
                xs, bs, ts, g_all, gi, w_in_all, conv0, state_lru_h[r], state_s5_re[r].reshape(bs, S5_N),
                state_s5_im[r].reshape(bs, S5_N), rec_prm, glu_w_all, glu_b_all, w_out_all, r)
            conv_p.append(jnp.transpose(cp.reshape(3, bp, W_LRU), (1, 0, 2)))
            conv_s.append(jnp.transpose(cs.reshape(3, bs, W_LRU), (1, 0, 2)))
            lru_p.append(lp); lru_s.append(ls)
            s5r_p.append(srp.reshape(bp, S5_GROUPS, S5_STATE)); s5r_s.append(srs.reshape(bs, S5_GROUPS, S5_STATE))
            s5i_p.append(sip.reshape(bp, S5_GROUPS, S5_STATE)); s5i_s.append(sis.reshape(bs, S5_GROUPS, S5_STATE))
        else:
            a = l // 2
            qb, qf, kt_p, vt_p, km = _qkv(xp, g_all, gi, wt_all, qg_all, kg_all, a, bp, kv_p)
            kv_p = (kt_p, vt_p)
            op = _attn_prompt(qb, qf, kt_p, vt_p, a, _group_block_means(km, bp, tp), bias_p)
            xp = _att_out(xp, op, w_o_all, a)
            xs_tb = _to_time_major(xs.reshape(bs, ts, D_MODEL))
            _, qf_s, kt_s, vt_s, _ = _qkv(xs_tb, g_all, gi, wt_all, qg_all, kg_all, a, 1, kv_s)
            kv_s = (kt_s, vt_s)
            q_s = jnp.transpose(qf_s.reshape(ts, bs, D_MODEL), (1, 0, 2))
            kn = jnp.transpose(kt_s[a, 0].reshape(D_MODEL, ts, bs), (2, 1, 0))
            vn = jnp.transpose(vt_s[a, 0].reshape(D_MODEL, ts, bs), (2, 1, 0))
            os_ = _attn_sample(a, page_table, q_s, kn, vn, bias_s, ckt, cvt)
            xs = _att_out(xs, os_.reshape(bs * ts, D_MODEL), w_o_all, a)
        xp = _ffn(xp, g_all, wg_all, wu_all, wd_all, l, 1)
        xs = _ffn(xs, g_all, wg_all, wu_all, wd_all, l, 1)

    n_att = depth // 2
    k_p = jnp.transpose(kv_p[0].reshape(n_att, bp, N_HEADS, HEAD_DIM, tp), (0, 1, 4, 2, 3))
    v_p = jnp.transpose(kv_p[1].reshape(n_att, bp, N_HEADS, HEAD_DIM, tp), (0, 1, 4, 2, 3))
    k_s = jnp.transpose(kv_s[0].reshape(n_att, N_HEADS, HEAD_DIM, ts, bs), (0, 4, 3, 1, 2))
    v_s = jnp.transpose(kv_s[1].reshape(n_att, N_HEADS, HEAD_DIM, ts, bs), (0, 4, 3, 1, 2))
    return (xp.reshape(bp, tp, D_MODEL), xs.reshape(bs, ts, D_MODEL), k_p, v_p, k_s, v_s,
            jnp.stack(lru_p), jnp.stack(lru_s), jnp.stack(conv_p), jnp.stack(conv_s),
            jnp.stack(s5r_p), jnp.stack(s5r_s), jnp.stack(s5i_p), jnp.stack(s5i_s))
```
